```python
import math
import jax, jax.numpy as jnp
from jax import lax
import numpy as np

D_MODEL = 1024
BATCH = 2
SEQ = 8192
DEPTH = 1
DEC_BATCH = 8
DEC_SEQ = 8192
PAST_LEN = 128

D_ATTN = D_MODEL // 2
ATTN_HEAD_DIM = 64
ATTN_HEADS = D_ATTN // ATTN_HEAD_DIM
DILATED_BRANCHES = ((128, 1), (512, 4), (2048, 16))
REL_BUCKETS = 32
REL_MAX_DIST = 1024
D_SSD = D_MODEL - D_ATTN
SSD_HEAD_DIM = 64
SSD_HEADS = D_SSD // SSD_HEAD_DIM
SSD_GROUPS = 2
SSD_STATE = 128
SSD_CONV = 5
SSD_CHUNK = 128
D_FF = -(-8 * D_MODEL // (3 * 256)) * 256
D_IN_PROJ = 3 * D_ATTN + 2 * D_SSD + 2 * SSD_GROUPS * SSD_STATE + 2 * SSD_HEADS
DEEPNORM_ALPHA = (2 * DEPTH) ** 0.25
DEEPNORM_BETA = (8 * DEPTH) ** -0.25
NORM_EPS = 1e-5

kernel_name = 'hymba_ssd_dilated_encoder'


def layer_norm(x, g, b):
    xf = x.astype(jnp.float32)
    mu = jnp.mean(xf, axis=-1, keepdims=True)
    var = jnp.mean(jnp.square(xf - mu), axis=-1, keepdims=True)
    return ((xf - mu) * lax.rsqrt(var + NORM_EPS) * g.astype(jnp.float32)
            + b.astype(jnp.float32)).astype(x.dtype)


def rms_norm(x, g, out_dtype):
    xf = x.astype(jnp.float32)
    inv = lax.rsqrt(jnp.mean(jnp.square(xf), axis=-1, keepdims=True) + NORM_EPS)
    return (xf * inv * g.astype(jnp.float32)).astype(out_dtype)


def t5_bucket(rel):
    half = REL_BUCKETS // 2
    max_exact = half // 2
    ret = (rel > 0).astype(np.int32) * half
    n = np.abs(rel)
    large = max_exact + (np.log(np.maximum(n, 1) / max_exact)
                         / math.log(REL_MAX_DIST / max_exact) * (half - max_exact)).astype(np.int32)
    large = np.minimum(large, half - 1)
    return ret + np.where(n < max_exact, n, large)


def dilated_branch(q, k, v, rel_table, window, dilation):
    b_, s, h, dh = q.shape
    r = window // (2 * dilation)
    blk = r
    l = s // dilation
    nb = -(-l // blk)
    lp = nb * blk
    n = b_ * dilation

    def to_sub(t):
        t = t.reshape(b_, l, dilation, h, dh).transpose(0, 2, 1, 3, 4).reshape(n, l, h, dh)
        return jnp.pad(t, ((0, 0), (0, lp - l), (0, 0), (0, 0)))

    def neighbours(t):
        tp = jnp.pad(t, ((0, 0), (blk, blk), (0, 0), (0, 0))).reshape(n, nb + 2, blk, h, dh)
        return jnp.concatenate([tp[:, :-2], tp[:, 1:-1], tp[:, 2:]], axis=2)

    qb = to_sub(q).reshape(n, nb, blk, h, dh)
    kw = neighbours(to_sub(k))
    vw = neighbours(to_sub(v))

    qi = np.arange(blk)[:, None]
    ti = np.arange(3 * blk)[None, :]
    rel = ti - blk - qi
    kidx = np.arange(nb)[:, None, None] * blk - blk + ti[None]
    valid = (np.abs(rel) <= r)[None] & (kidx >= 0) & (kidx < l)
    bias = rel_table[t5_bucket(rel * dilation)]
    bias = jnp.transpose(bias, (2, 0, 1)).astype(jnp.float32)

    logits = jnp.einsum('nbqhd,nbkhd->nbhqk', qb, kw).astype(jnp.float32) + bias[None, None]
    logits = jnp.where(valid[None, :, None], logits, -jnp.inf)
    m = jnp.max(logits, axis=-1, keepdims=True)
    p = jnp.exp(logits - m)
    denom = jnp.sum(p, axis=-1, keepdims=True)
    o = jnp.einsum('nbhqk,nbkhd->nbqhd', p / denom, vw.astype(jnp.float32))
    lse = (m + jnp.log(denom))[..., 0]

    o = o.reshape(n, lp, h, dh)[:, :l].reshape(b_, dilation, l, h, dh)
    o = o.transpose(0, 2, 1, 3, 4).reshape(b_, s, h, dh)
    lse = lse.transpose(0, 1, 3, 2).reshape(n, lp, h)[:, :l].reshape(b_, dilation, l, h)
    lse = lse.transpose(0, 2, 1, 3).reshape(b_, s, h)
    return o, lse


def dilated_attention(q, k, v, rel_table):
    outs, lses = [], []
    for window, dilation in DILATED_BRANCHES:
        o, lse = dilated_branch(q, k, v, rel_table, window, dilation)
        outs.append(o)
        lses.append(lse)
    w = jax.nn.softmax(jnp.stack(lses, axis=0), axis=0)
    return jnp.sum(w[..., None] * jnp.stack(outs, axis=0), axis=0)


def centred_depthwise_conv(u, w, b):
    kw = w.shape[0]
    out = lax.conv_general_dilated(
        u, w[:, None, :].astype(u.dtype), window_strides=(1,),
        padding=[(kw // 2, kw // 2)], dimension_numbers=('NWC', 'WIO', 'NWC'),
        feature_group_count=u.shape[-1])
    return out + b


def ssd_chunked(xs, dt, a, bm, cm):
    b_, l, h, p = xs.shape
    g, n = bm.shape[2], bm.shape[3]
    e = h // g
    qc = SSD_CHUNK
    c = l // qc
    f32 = jnp.float32
    xdt = (xs.astype(f32) * dt[..., None]).reshape(b_, c, qc, g, e, p)
    da = (dt * a.astype(f32)).reshape(b_, c, qc, g, e)
    a_cum = jnp.cumsum(da, axis=2).transpose(0, 1, 3, 4, 2)
    bm = bm.astype(f32).reshape(b_, c, qc, g, n)
    cm = cm.astype(f32).reshape(b_, c, qc, g, n)

    lower = np.tril(np.ones((qc, qc), dtype=bool))
    seg = a_cum[..., :, None] - a_cum[..., None, :]
    decay = jnp.exp(jnp.where(lower, seg, -jnp.inf))
    cb = jnp.einsum('bcqgn,bcsgn->bcgqs', cm, bm)
    y_diag = jnp.einsum('bcgeqs,bcsgep->bcqgep', cb[:, :, :, None] * decay, xdt)

    decay_states = jnp.exp(a_cum[..., -1:] - a_cum)
    states = jnp.einsum('bcqgn,bcgeq,bcqgep->bcgepn', bm, decay_states, xdt)
    chunk_decay = jnp.exp(a_cum[..., -1])

    def step(carry, inp):
        st, dec = inp
        return dec[..., None, None] * carry + st, carry

    init = jnp.zeros((b_, g, e, p, n), f32)
    _, prev = lax.scan(step, init, (jnp.moveaxis(states, 1, 0), jnp.moveaxis(chunk_decay, 1, 0)))
    prev = jnp.moveaxis(prev, 0, 1)
    y_off = jnp.einsum('bcqgn,bcgepn,bcgeq->bcqgep', cm, prev, jnp.exp(a_cum))
    return (y_diag + y_off).reshape(b_, l, h, p)


def encoder_layer(x, rel_bias, w_in, conv_w, conv_b, dt_bias_fwd, dt_bias_bwd,
                  a_log_fwd, a_log_bwd, d_skip, attn_norm_g, ssd_norm_g, w_out,
                  ln1_g, ln1_b, w_gate, w_up, w_down, ln2_g, ln2_b):
    b_, s, _ = x.shape
    f32 = jnp.float32
    proj = jnp.einsum('bsd,de->bse', x, w_in)
    split_at = [D_ATTN, 2 * D_ATTN, 3 * D_ATTN, 3 * D_ATTN + D_SSD,
                3 * D_ATTN + 2 * D_SSD + 2 * SSD_GROUPS * SSD_STATE]
    q, k, v, z, xbc, dt_raw = jnp.split(proj, split_at, axis=-1)

    def heads(t):
        return t.reshape(b_, s, ATTN_HEADS, ATTN_HEAD_DIM)
    attn = dilated_attention(heads(q) * (ATTN_HEAD_DIM ** -0.5), heads(k), heads(v), rel_bias)
    attn = rms_norm(attn.reshape(b_, s, D_ATTN), attn_norm_g, x.dtype)

    xbc = jax.nn.silu(centred_depthwise_conv(xbc, conv_w, conv_b))
    xs, bm, cm = jnp.split(xbc, [D_SSD, D_SSD + SSD_GROUPS * SSD_STATE], axis=-1)
    xs = xs.reshape(b_, s, SSD_HEADS, SSD_HEAD_DIM)
    bm = bm.reshape(b_, s, SSD_GROUPS, SSD_STATE)
    cm = cm.reshape(b_, s, SSD_GROUPS, SSD_STATE)
    dt = jax.nn.softplus(dt_raw.astype(f32)
                         + jnp.concatenate([dt_bias_fwd, dt_bias_bwd]).astype(f32))
    dt_f, dt_b = dt[..., :SSD_HEADS], dt[..., SSD_HEADS:]

    def flip(t):
        return jnp.flip(t, axis=1)
    y_f = ssd_chunked(xs, dt_f, -jnp.exp(a_log_fwd.astype(f32)), bm, cm)
    y_b = flip(ssd_chunked(flip(xs), flip(dt_b), -jnp.exp(a_log_bwd.astype(f32)), flip(bm), flip(cm)))
    y = y_f + y_b + d_skip.astype(f32)[:, None] * xs.astype(f32)
    y = rms_norm(y.reshape(b_, s, D_SSD) * jax.nn.silu(z.astype(f32)), ssd_norm_g, x.dtype)

    mix = jnp.einsum('bse,ed->bsd', jnp.concatenate([attn, y], axis=-1), w_out)
    x = layer_norm(DEEPNORM_ALPHA * x + mix, ln1_g, ln1_b)

    hidden = jax.nn.silu(jnp.einsum('bsd,df->bsf', x, w_gate)) * jnp.einsum('bsd,df->bsf', x, w_up)
    ffn = jnp.einsum('bsf,fd->bsd', hidden, w_down)
    return layer_norm(DEEPNORM_ALPHA * x + ffn, ln2_g, ln2_b)


def encoder_trunk(x, rel_bias, layer_params):
    for layer in range(DEPTH):
        x = encoder_layer(x, rel_bias, *[prm[layer] for prm in layer_params])
    return x


def setup_inputs(seed: int = 0) -> dict:
    key = jax.random.key(seed)
    ks = jax.random.split(key, 32)
    f32 = jnp.float32

    def nrm(k, shape, scale):
        return jax.random.normal(k, shape, f32) * scale

    s_in = D_MODEL ** -0.5
    x_prompt = jax.random.normal(ks[0], (BATCH, SEQ, D_MODEL), f32)
    x_sample = jax.random.normal(ks[1], (DEC_BATCH, DEC_SEQ, D_MODEL), f32)
    rel_bias = nrm(ks[2], (REL_BUCKETS, ATTN_HEADS), 0.5)
    w_in = jnp.concatenate([
        nrm(ks[3], (DEPTH, D_MODEL, D_ATTN), s_in),
        nrm(ks[4], (DEPTH, D_MODEL, D_ATTN), s_in),
        nrm(ks[5], (DEPTH, D_MODEL, D_ATTN), s_in * DEEPNORM_BETA),
        nrm(ks[6], (DEPTH, D_MODEL, D_SSD), s_in),
        nrm(ks[7], (DEPTH, D_MODEL, D_SSD), s_in * DEEPNORM_BETA),
        nrm(ks[8], (DEPTH, D_MODEL, SSD_GROUPS * SSD_STATE), s_in),
        nrm(ks[9], (DEPTH, D_MODEL, SSD_GROUPS * SSD_STATE), s_in),
        nrm(ks[10], (DEPTH, D_MODEL, 2 * SSD_HEADS), s_in * 0.1),
    ], axis=-1)
    conv_ch = D_SSD + 2 * SSD_GROUPS * SSD_STATE
    conv_w = nrm(ks[11], (DEPTH, SSD_CONV, conv_ch), SSD_CONV ** -0.5)
    conv_b = nrm(ks[12], (DEPTH, conv_ch), 0.01)

    def dt_bias_init(k):
        dt0 = jnp.exp(jax.random.uniform(k, (DEPTH, SSD_HEADS), f32,
                                         minval=math.log(1e-3), maxval=math.log(1e-1)))
        return dt0 + jnp.log(-jnp.expm1(-dt0))

    dt_bias_fwd = dt_bias_init(ks[13])
    dt_bias_bwd = dt_bias_init(ks[14])
    a_log_fwd = jnp.log(jax.random.uniform(ks[15], (DEPTH, SSD_HEADS), f32, minval=1.0, maxval=16.0))
    a_log_bwd = jnp.log(jax.random.uniform(ks[16], (DEPTH, SSD_HEADS), f32, minval=1.0, maxval=16.0))
    d_skip = 1.0 + nrm(ks[17], (DEPTH, SSD_HEADS), 0.1)
    attn_norm_g = 1.0 + nrm(ks[18], (DEPTH, D_ATTN), 0.01)
    ssd_norm_g = 1.0 + nrm(ks[19], (DEPTH, D_SSD), 0.01)
    w_out = nrm(ks[20], (DEPTH, D_MODEL, D_MODEL), s_in * DEEPNORM_BETA)
    ln1_g = 1.0 + nrm(ks[21], (DEPTH, D_MODEL), 0.01)
    ln1_b = nrm(ks[22], (DEPTH, D_MODEL), 0.01)
    w_gate = nrm(ks[23], (DEPTH, D_MODEL, D_FF), s_in * DEEPNORM_BETA)
    w_up = nrm(ks[24], (DEPTH, D_MODEL, D_FF), s_in * DEEPNORM_BETA)
    w_down = nrm(ks[25], (DEPTH, D_FF, D_MODEL), D_FF ** -0.5 * DEEPNORM_BETA)
    ln2_g = 1.0 + nrm(ks[26], (DEPTH, D_MODEL), 0.01)
    ln2_b = nrm(ks[27], (DEPTH, D_MODEL), 0.01)
    return {'x_prompt': x_prompt, 'x_sample': x_sample, 'rel_bias': rel_bias,
            'w_in': w_in, 'conv_w': conv_w, 'conv_b': conv_b,
            'dt_bias_fwd': dt_bias_fwd, 'dt_bias_bwd': dt_bias_bwd,
            'a_log_fwd': a_log_fwd, 'a_log_bwd': a_log_bwd, 'd_skip': d_skip,
            'attn_norm_g': attn_norm_g, 'ssd_norm_g': ssd_norm_g, 'w_out': w_out,
            'ln1_g': ln1_g, 'ln1_b': ln1_b, 'w_gate': w_gate, 'w_up': w_up,
            'w_down': w_down, 'ln2_g': ln2_g, 'ln2_b': ln2_b}


def reference(x_prompt, x_sample, rel_bias, w_in, conv_w, conv_b, dt_bias_fwd, dt_bias_bwd,
              a_log_fwd, a_log_bwd, d_skip, attn_norm_g, ssd_norm_g, w_out,
              ln1_g, ln1_b, w_gate, w_up, w_down, ln2_g, ln2_b):
    layer_params = (w_in, conv_w, conv_b, dt_bias_fwd, dt_bias_bwd, a_log_fwd, a_log_bwd,
                    d_skip, attn_norm_g, ssd_norm_g, w_out, ln1_g, ln1_b,
                    w_gate, w_up, w_down, ln2_g, ln2_b)
    y_prompt = encoder_trunk(x_prompt, rel_bias, layer_params)
    y_sample = encoder_trunk(x_sample, rel_bias, layer_params)
    return (y_prompt, y_sample)
```

```python
import functools
import math

import jax
import jax.numpy as jnp
import numpy as np
from jax import lax
from jax.experimental import pallas as pl
from jax.experimental.pallas import tpu as pltpu

F32 = jnp.float32
BF16 = jnp.bfloat16

D_MODEL = 1024
D_ATTN = 512
HEAD_DIM = 64
N_HEADS = 8
D_SSD = 512
SSD_GROUPS = 2
SSD_STATE = 128
SSD_CONV = 5
CHUNK = 128
D_FF = 2816
D_XBC = D_SSD + 2 * SSD_GROUPS * SSD_STATE
DILATIONS = (1, 4, 16)
SIDE = 64
REL_BUCKETS = 32
REL_MAX_DIST = 1024
DEPTH = 1
ALPHA = (2 * DEPTH) ** 0.25
EPS = 1e-5
NEG = -1e30

LANES = 128
HALO = 8
VMEM_LIMIT = 56 * 1024 * 1024

TM_PROJ = 512
R_SSD = 512
RB_ATT = 512
QB = 128
TC_MERGE = 512
TM_FFN = 512
FF_CHUNK = 1408


def _const_spec(shape):
    nd = len(shape)
    return pl.BlockSpec(shape, lambda *_: (0,) * nd, pipeline_mode=pl.Buffered(1))


def _params(n_axes):
    return pltpu.CompilerParams(dimension_semantics=("arbitrary",) * n_axes,
                                vmem_limit_bytes=VMEM_LIMIT)


def _sigmoid(x):
    return 1.0 / (1.0 + jnp.exp(-x))


def _softplus(x):
    return jnp.maximum(x, 0.0) + jnp.log1p(jnp.exp(-jnp.abs(x)))


def _split3(x):
    hi = x.astype(BF16)
    r1 = x - hi.astype(F32)
    mid = r1.astype(BF16)
    lo = (r1 - mid.astype(F32)).astype(BF16)
    return hi, mid, lo


def _exact_dot_r(sel, x):
    hi, mid, lo = _split3(x)
    d = functools.partial(jnp.dot, preferred_element_type=F32)
    return d(sel, hi) + d(sel, mid) + d(sel, lo)


def _exact_dot_l(x, sel):
    hi, mid, lo = _split3(x)
    d = functools.partial(jnp.dot, preferred_element_type=F32)
    return d(hi, sel) + d(mid, sel) + d(lo, sel)


def _lane_lt64(shape):
    return lax.broadcasted_iota(jnp.int32, shape, len(shape) - 1) < HEAD_DIM


def _in_proj_kernel(x_ref, wqkv_ref, wz_ref, wxbc_ref, wdt_ref, wdtt_ref, dtb_ref, dtbt_ref,
                    q1_ref, k1_ref, v1_ref, q4_ref, k4_ref, v4_ref, q16_ref, k16_ref, v16_ref,
                    z_ref, xbc_ref, dt_ref, dtt_ref, slab_ref, *, tm):
    xb = x_ref[...].astype(BF16)
    dot = functools.partial(jnp.dot, preferred_element_type=F32)
    z_ref[...] = dot(xb, wz_ref[...])
    xbc_ref[...] = dot(xb, wxbc_ref[...])
    dt_ref[...] = _softplus(dot(xb, wdt_ref[...]) + dtb_ref[...])
    dtt = lax.dot_general(wdtt_ref[...], xb, (((1,), (1,)), ((), ())), preferred_element_type=F32)
    dtt = _softplus(dtt + dtbt_ref[:, 0:1])
    for c in range(tm // CHUNK):
        dtt_ref[c] = dtt[:, c * CHUNK:(c + 1) * CHUNK]

    qkv = dot(xb, wqkv_ref[...])
    n_slab = 3 * D_ATTN // LANES
    for s in range(n_slab):
        col = qkv[:, s * LANES:(s + 1) * LANES]
        if s < D_ATTN // LANES:
            col = col * (HEAD_DIM ** -0.5)
        slab_ref[s] = col
    per = D_ATTN // LANES
    nat = (q1_ref, k1_ref, v1_ref)
    d4 = (q4_ref, k4_ref, v4_ref)
    d16 = (q16_ref, k16_ref, v16_ref)
    for s in range(n_slab):
        a, c = divmod(s, per)
        lanes = slice(c * LANES, (c + 1) * LANES)
        nat[a][:, lanes] = slab_ref[s].astype(BF16)
        for r in range(4):
            d4[a][r, :, lanes] = slab_ref[s, pl.ds(r, tm // 4, stride=4), :].astype(BF16)
        for r in range(16):
            d16[a][r, :, lanes] = slab_ref[s, pl.ds(r, tm // 16, stride=16), :].astype(BF16)


def _in_proj(x, wqkv, wz, wxbc, wdt, wdtt, dtb, dtbt):
    nseq, seq, _ = x.shape
    tm = min(TM_PROJ, seq)
    grid = (nseq, seq // tm)
    row = lambda w: pl.BlockSpec((None, tm, w), lambda n, j: (n, j, 0))
    res = lambda d: pl.BlockSpec((None, d, tm // d, D_ATTN), lambda n, j: (n, 0, j, 0))
    sds = jax.ShapeDtypeStruct
    out_shape = (
        [sds((nseq, seq, D_ATTN), BF16)] * 3
        + [sds((nseq, 4, seq // 4, D_ATTN), BF16)] * 3
        + [sds((nseq, 16, seq // 16, D_ATTN), BF16)] * 3
        + [sds((nseq, seq, D_SSD), F32), sds((nseq, seq, D_XBC), F32), sds((nseq, seq, LANES), F32),
           sds((nseq, seq // CHUNK, 16, CHUNK), F32)]
    )
    out_specs = (
        [row(D_ATTN)] * 3 + [res(4)] * 3 + [res(16)] * 3
        + [row(D_SSD), row(D_XBC), row(LANES),
           pl.BlockSpec((None, tm // CHUNK, 16, CHUNK), lambda n, j: (n, j, 0, 0))]
    )
    in_specs = [row(D_MODEL)] + [_const_spec(w.shape) for w in (wqkv, wz, wxbc, wdt, wdtt, dtb, dtbt)]
    return pl.pallas_call(
        functools.partial(_in_proj_kernel, tm=tm),
        grid=grid, in_specs=in_specs, out_specs=out_specs, out_shape=out_shape,
        scratch_shapes=[pltpu.VMEM((3 * D_ATTN // LANES, tm, LANES), F32)],
        compiler_params=_params(2), name="in_proj",
    )(x, wqkv, wz, wxbc, wdt, wdtt, dtb, dtbt)


def _ssd_kernel(*refs, rows, nblk, backward):
    if backward:
        (xprev_ref, xcur_ref, xnext_ref, dt_ref, dtt_ref, convw_ref, convb_ref, alog_ref, alogt_ref,
         tcol_ref, trow_ref, e128_ref, e64_ref, yf_ref, z_ref, dskip_ref, gnorm_ref,
         out_ref, ext_ref, act_ref, state_ref) = refs
    else:
        (xprev_ref, xcur_ref, xnext_ref, dt_ref, dtt_ref, convw_ref, convb_ref, alog_ref, alogt_ref,
         tcol_ref, trow_ref, e128_ref, e64_ref,
         out_ref, ext_ref, act_ref, state_ref) = refs
    j = pl.program_id(1)
    blk = (nblk - 1 - j) if backward else j
    nchunk = rows // CHUNK
    head0 = N_HEADS if backward else 0

    @pl.when(j == 0)
    def _():
        state_ref[...] = jnp.zeros_like(state_ref)

    keep_prev = jnp.where(blk == 0, 0.0, 1.0)
    keep_next = jnp.where(blk == nblk - 1, 0.0, 1.0)
    ext_ref[0:HALO, :] = xprev_ref[...] * keep_prev
    ext_ref[HALO:HALO + rows, :] = xcur_ref[...]
    ext_ref[HALO + rows:2 * HALO + rows, :] = xnext_ref[...] * keep_next
    for c in range(nchunk):
        for half in range(D_XBC // 512):
            lanes = slice(half * 512, (half + 1) * 512)
            acc = jnp.broadcast_to(convb_ref[:, lanes], (CHUNK, 512))
            for k in range(SSD_CONV):
                start = HALO + c * CHUNK + k - SSD_CONV // 2
                acc = acc + convw_ref[k:k + 1, lanes] * ext_ref[start:start + CHUNK, lanes]
            act_ref[c * CHUNK:(c + 1) * CHUNK, lanes] = acc * _sigmoid(acc)

    a_row = -jnp.exp(alog_ref[...])
    a_col = -jnp.exp(alogt_ref[:, 0:1])
    qi = lax.broadcasted_iota(jnp.int32, (CHUNK, CHUNK), 0)
    si = lax.broadcasted_iota(jnp.int32, (CHUNK, CHUNK), 1)
    causal = (si >= qi) if backward else (si <= qi)
    tot_row = 0 if backward else CHUNK - 1
    lt64 = _lane_lt64((CHUNK, LANES))
    dot = functools.partial(jnp.dot, preferred_element_type=F32)

    def chunk_body(ci, carry):
        cc = (nchunk - 1 - ci) if backward else ci
        r0 = pl.multiple_of(cc * CHUNK, CHUNK)
        rs = pl.ds(r0, CHUNK)
        xs = act_ref[rs, 0:D_SSD]
        bm = [act_ref[rs, D_SSD + g * SSD_STATE:D_SSD + (g + 1) * SSD_STATE].astype(BF16)
              for g in range(SSD_GROUPS)]
        cm32 = [act_ref[rs, D_SSD + (SSD_GROUPS + g) * SSD_STATE:D_SSD + (SSD_GROUPS + g + 1) * SSD_STATE]
                for g in range(SSD_GROUPS)]
        dt = dt_ref[rs, :]
        dtt = dtt_ref[cc]
        acum = _exact_dot_r(tcol_ref[...], dt * a_row)
        acum_t = _exact_dot_l(dtt * a_col, trow_ref[...])
        acol = _exact_dot_l(acum, e128_ref[...])
        dt64 = _exact_dot_l(dt, e64_ref[...])
        xdt = xs * dt64
        xdt_b = xdt.astype(BF16)

        y_tiles = []
        xw_tiles = []
        cd_tiles = []
        for p in range(N_HEADS // 2):
            g = p // 2
            a0 = acol[:, (2 * p) * LANES:(2 * p + 1) * LANES]
            a1 = acol[:, (2 * p + 1) * LANES:(2 * p + 2) * LANES]
            a64 = jnp.where(lt64, a0, a1)
            tot = a64[tot_row:tot_row + 1, :]
            xw_tiles.append((xdt[:, p * LANES:(p + 1) * LANES] * jnp.exp(tot - a64)).astype(BF16))
            cd_tiles.append(jnp.exp(tot))
            cb = lax.dot_general(cm32[g].astype(BF16), bm[g], (((1,), (1,)), ((), ())),
                                 preferred_element_type=F32)
            s_pair = state_ref[g, :, (p % 2) * LANES:(p % 2 + 1) * LANES].astype(BF16)
            rhs = jnp.concatenate([xdt_b[:, p * LANES:(p + 1) * LANES], s_pair], axis=0)
            ys = []
            for h, ah in ((2 * p, a0), (2 * p + 1, a1)):
                arow = jnp.broadcast_to(acum_t[head0 + h:head0 + h + 1, :], (CHUNK, CHUNK))
                decay = jnp.exp(jnp.where(causal, ah - arow, -jnp.inf))
                lmat = (cb * decay).astype(BF16)
                cs = (cm32[g] * jnp.exp(ah)).astype(BF16)
                ys.append(dot(jnp.concatenate([lmat, cs], axis=1), rhs))
            y_tiles.append(jnp.where(lt64, ys[0], ys[1]))
        for g in range(SSD_GROUPS):
            xw = jnp.concatenate(xw_tiles[2 * g:2 * g + 2], axis=1)
            cd = jnp.concatenate(cd_tiles[2 * g:2 * g + 2], axis=1)
            upd = lax.dot_general(bm[g], xw, (((0,), (0,)), ((), ())), preferred_element_type=F32)
            state_ref[g] = cd * state_ref[g] + upd
        y = jnp.concatenate(y_tiles, axis=1)
        if backward:
            zz = z_ref[rs, :]
            tot_y = (yf_ref[rs, :] + y + dskip_ref[...] * xs) * (zz * _sigmoid(zz))
            inv = lax.rsqrt(jnp.mean(tot_y * tot_y, axis=-1, keepdims=True) + EPS)
            out_ref[rs, :] = (tot_y * inv * gnorm_ref[...]).astype(out_ref.dtype)
        else:
            out_ref[rs, :] = y
        return carry

    lax.fori_loop(0, nchunk, chunk_body, 0)


def _ssd_pass(xbc, dt, dtt, consts, extra, *, backward):
    nseq, seq, _ = xbc.shape
    rows = min(R_SSD, seq)
    nblk = seq // rows
    hb = rows // HALO
    nh = seq // HALO
    blk = (lambda j: nblk - 1 - j) if backward else (lambda j: j)
    cur = lambda w: pl.BlockSpec((None, rows, w), lambda n, j: (n, blk(j), 0))
    in_specs = [
        pl.BlockSpec((None, HALO, D_XBC), lambda n, j: (n, jnp.maximum(blk(j) * hb - 1, 0), 0)),
        cur(D_XBC),
        pl.BlockSpec((None, HALO, D_XBC), lambda n, j: (n, jnp.minimum((blk(j) + 1) * hb, nh - 1), 0)),
        cur(LANES),
        pl.BlockSpec((None, rows // CHUNK, 16, CHUNK), lambda n, j: (n, blk(j), 0, 0)),
    ] + [_const_spec(c.shape) for c in consts]
    args = [xbc, xbc, xbc, dt, dtt, *consts]
    if backward:
        yf, z, dskip, gnorm = extra
        in_specs += [cur(D_SSD), cur(D_SSD), _const_spec(dskip.shape), _const_spec(gnorm.shape)]
        args += [yf, z, dskip, gnorm]
    out_dtype = BF16 if backward else F32
    return pl.pallas_call(
        functools.partial(_ssd_kernel, rows=rows, nblk=nblk, backward=backward),
        grid=(nseq, nblk), in_specs=in_specs, out_specs=cur(D_SSD),
        out_shape=jax.ShapeDtypeStruct((nseq, seq, D_SSD), out_dtype),
        scratch_shapes=[pltpu.VMEM((rows + 2 * HALO, D_XBC), F32),
                        pltpu.VMEM((rows, D_XBC), F32),
                        pltpu.VMEM((SSD_GROUPS, SSD_STATE, 2 * LANES), F32)],
        compiler_params=_params(2), name="ssd_bwd" if backward else "ssd_fwd",
    )(*args)


def _attn_kernel(q_ref, kprev_ref, kcur_ref, knext_ref, vprev_ref, vcur_ref, vnext_ref, bias_ref,
                 o_ref, lse_ref, kbuf_ref, vbuf_ref, *, rows, nblk):
    j = pl.program_id(1)
    first = j == 0
    last = j == nblk - 1
    for buf, prev, cur, nxt in ((kbuf_ref, kprev_ref, kcur_ref, knext_ref),
                                (vbuf_ref, vprev_ref, vcur_ref, vnext_ref)):
        buf[0:SIDE, :] = prev[...]
        buf[SIDE:SIDE + rows, :] = cur[...]
        buf[SIDE + rows:2 * SIDE + rows, :] = nxt[...]
    win = QB + 2 * SIDE
    lt64 = _lane_lt64((QB, LANES))
    lane = lax.broadcasted_iota(jnp.int32, (QB, LANES), 1)
    col = lax.broadcasted_iota(jnp.int32, (2 * QB, win), 1)
    nqb = rows // QB
    for qb in range(nqb):
        qrows = slice(qb * QB, (qb + 1) * QB)
        wrows = slice(qb * QB, qb * QB + win)
        lse_pk = jnp.zeros((QB, LANES), F32)
        for p in range(N_HEADS // 2):
            lanes = slice(p * LANES, (p + 1) * LANES)
            qp = q_ref[qrows, lanes]
            zero = jnp.zeros_like(qp)
            q2 = jnp.concatenate([jnp.where(lt64, qp, zero), jnp.where(lt64, zero, qp)], axis=0)
            s = lax.dot_general(q2, kbuf_ref[wrows, lanes], (((1,), (1,)), ((), ())),
                                preferred_element_type=F32) + bias_ref[p]
            if qb == 0:
                s = jnp.where(jnp.logical_and(col < SIDE, first), NEG, s)
            if qb == nqb - 1:
                s = jnp.where(jnp.logical_and(col >= win - SIDE, last), NEG, s)
            m = jnp.max(s, axis=1, keepdims=True)
            e = jnp.exp(s - m)
            l = jnp.sum(e, axis=1, keepdims=True)
            o2 = jnp.dot(e.astype(BF16), vbuf_ref[wrows, lanes], preferred_element_type=F32)
            o2 = o2 * (1.0 / l)
            o_ref[qrows, lanes] = jnp.where(lt64, o2[0:QB], o2[QB:2 * QB])
            lse2 = m + jnp.log(l)
            lse_pk = (lse_pk + jnp.where(lane == 2 * p, lse2[0:QB], 0.0)
                      + jnp.where(lane == 2 * p + 1, lse2[QB:2 * QB], 0.0))
        lse_ref[qrows, :] = lse_pk


def _attn_branch(q, k, v, bias):
    ng, length, _ = q.shape
    rows = min(RB_ATT, length)
    nblk = length // rows
    hb = rows // SIDE
    nh = length // SIDE
    cur = lambda w: pl.BlockSpec((None, rows, w), lambda g, j: (g, j, 0))
    prev = pl.BlockSpec((None, SIDE, D_ATTN), lambda g, j: (g, jnp.maximum(j * hb - 1, 0), 0))
    nxt = pl.BlockSpec((None, SIDE, D_ATTN), lambda g, j: (g, jnp.minimum((j + 1) * hb, nh - 1), 0))
    return pl.pallas_call(
        functools.partial(_attn_kernel, rows=rows, nblk=nblk),
        grid=(ng, nblk),
        in_specs=[cur(D_ATTN), prev, cur(D_ATTN), nxt, prev, cur(D_ATTN), nxt, _const_spec(bias.shape)],
        out_specs=[cur(D_ATTN), cur(LANES)],
        out_shape=[jax.ShapeDtypeStruct((ng, length, D_ATTN), F32),
                   jax.ShapeDtypeStruct((ng, length, LANES), F32)],
        scratch_shapes=[pltpu.VMEM((rows + 2 * SIDE, D_ATTN), BF16),
                        pltpu.VMEM((rows + 2 * SIDE, D_ATTN), BF16)],
        compiler_params=_params(2), name="dilated_attn",
    )(q, k, k, k, v, v, v, bias)


def _merge_kernel(o1_ref, o4_ref, o16_ref, l1_ref, l4_ref, l16_ref, e64_ref, gnorm_ref, out_ref,
                  so4_ref, so16_ref, sl4_ref, sl16_ref, *, tc):
    per = D_ATTN // LANES
    for d, o_ref, l_ref, so_ref, sl_ref in ((4, o4_ref, l4_ref, so4_ref, sl4_ref),
                                            (16, o16_ref, l16_ref, so16_ref, sl16_ref)):
        for r in range(d):
            rows = pl.ds(r, tc // d, stride=d)
            sl_ref[rows, :] = l_ref[r]
            for s in range(per):
                so_ref[s, rows, :] = o_ref[r, :, s * LANES:(s + 1) * LANES]
    for b in range(tc // QB):
        rs = slice(b * QB, (b + 1) * QB)
        l1, l4, l16 = l1_ref[rs, :], sl4_ref[rs, :], sl16_ref[rs, :]
        m = jnp.maximum(jnp.maximum(l1, l4), l16)
        e1, e4, e16 = jnp.exp(l1 - m), jnp.exp(l4 - m), jnp.exp(l16 - m)
        inv = 1.0 / (e1 + e4 + e16)
        w1 = _exact_dot_l(e1 * inv, e64_ref[...])
        w4 = _exact_dot_l(e4 * inv, e64_ref[...])
        w16 = _exact_dot_l(e16 * inv, e64_ref[...])
        o4 = jnp.concatenate([so4_ref[s, rs, :] for s in range(per)], axis=1)
        o16 = jnp.concatenate([so16_ref[s, rs, :] for s in range(per)], axis=1)
        o = w1 * o1_ref[rs, :] + w4 * o4 + w16 * o16
        inv_rms = lax.rsqrt(jnp.mean(o * o, axis=-1, keepdims=True) + EPS)
        out_ref[rs, :] = (o * inv_rms * gnorm_ref[...]).astype(out_ref.dtype)


def _merge(o1, o4, o16, l1, l4, l16, e64, gnorm):
    nseq, seq, _ = o1.shape
    tc = min(TC_MERGE, seq)
    row = lambda w: pl.BlockSpec((None, tc, w), lambda n, j: (n, j, 0))
    res = lambda d, w: pl.BlockSpec((None, d, tc // d, w), lambda n, j: (n, 0, j, 0))
    return pl.pallas_call(
        functools.partial(_merge_kernel, tc=tc),
        grid=(nseq, seq // tc),
        in_specs=[row(D_ATTN), res(4, D_ATTN), res(16, D_ATTN), row(LANES), res(4, LANES), res(16, LANES),
                  _const_spec(e64.shape), _const_spec(gnorm.shape)],
        out_specs=row(D_ATTN),
        out_shape=jax.ShapeDtypeStruct((nseq, seq, D_ATTN), BF16),
        scratch_shapes=[pltpu.VMEM((D_ATTN // LANES, tc, LANES), F32),
                        pltpu.VMEM((D_ATTN // LANES, tc, LANES), F32),
                        pltpu.VMEM((tc, LANES), F32), pltpu.VMEM((tc, LANES), F32)],
        compiler_params=_params(2), name="merge",
    )(o1, o4, o16, l1, l4, l16, e64, gnorm)


def _layer_norm(x, g, b):
    mu = jnp.mean(x, axis=-1, keepdims=True)
    xc = x - mu
    var = jnp.mean(xc * xc, axis=-1, keepdims=True)
    return xc * lax.rsqrt(var + EPS) * g + b


def _ffn_kernel(x_ref, a_ref, y_ref, woa_ref, woy_ref, g1_ref, b1_ref, wg_ref, wu_ref, wd_ref,
                g2_ref, b2_ref, out_ref):
    dot = functools.partial(jnp.dot, preferred_element_type=F32)
    mix = dot(a_ref[...], woa_ref[...]) + dot(y_ref[...], woy_ref[...])
    h1 = _layer_norm(ALPHA * x_ref[...] + mix, g1_ref[...], b1_ref[...])
    h1b = h1.astype(BF16)
    ffn = jnp.zeros_like(h1)
    for c in range(D_FF // FF_CHUNK):
        cols = slice(c * FF_CHUNK, (c + 1) * FF_CHUNK)
        gate = dot(h1b, wg_ref[:, cols])
        up = dot(h1b, wu_ref[:, cols])
        hid = (gate * _sigmoid(gate) * up).astype(BF16)
        ffn = ffn + dot(hid, wd_ref[cols, :])
    out_ref[...] = _layer_norm(ALPHA * h1 + ffn, g2_ref[...], b2_ref[...])


def _out_ffn(x, attn, y, woa, woy, g1, b1, wg, wu, wd, g2, b2):
    nseq, seq, _ = x.shape
    tm = min(TM_FFN, seq)
    row = lambda w: pl.BlockSpec((None, tm, w), lambda n, j: (n, j, 0))
    consts = (woa, woy, g1, b1, wg, wu, wd, g2, b2)
    return pl.pallas_call(
        _ffn_kernel,
        grid=(nseq, seq // tm),
        in_specs=[row(D_MODEL), row(D_ATTN), row(D_SSD)] + [_const_spec(c.shape) for c in consts],
        out_specs=row(D_MODEL),
        out_shape=jax.ShapeDtypeStruct((nseq, seq, D_MODEL), F32),
        compiler_params=_params(2), name="out_ffn",
    )(x, attn, y, *consts)


def _t5_bucket(rel):
    half = REL_BUCKETS // 2
    max_exact = half // 2
    ret = (rel > 0).astype(np.int32) * half
    n = np.abs(rel)
    large = max_exact + (np.log(np.maximum(n, 1) / max_exact)
                         / math.log(REL_MAX_DIST / max_exact) * (half - max_exact)).astype(np.int32)
    large = np.minimum(large, half - 1)
    return ret + np.where(n < max_exact, n, large)


def _bias_tables(rel_bias):
    qi = np.arange(QB)[:, None]
    ti = np.arange(QB + 2 * SIDE)[None, :]
    rel = ti - SIDE - qi
    valid = np.abs(rel) <= SIDE
    tables = []
    for d in DILATIONS:
        bias = rel_bias.astype(F32)[_t5_bucket(rel * d)]
        bias = jnp.where(valid[:, :, None], bias, NEG)
        bias = jnp.transpose(bias, (2, 0, 1))
        tables.append(bias.reshape(N_HEADS // 2, 2 * QB, QB + 2 * SIDE))
    return tables


def _selector_constants():
    tri = np.tril(np.ones((CHUNK, CHUNK), np.float32))
    e128 = np.zeros((2, LANES, N_HEADS * LANES), np.float32)
    e64 = np.zeros((2, LANES, D_SSD), np.float32)
    for direction in range(2):
        for h in range(N_HEADS):
            e128[direction, direction * N_HEADS + h, h * LANES:(h + 1) * LANES] = 1.0
            e64[direction, direction * N_HEADS + h, h * HEAD_DIM:(h + 1) * HEAD_DIM] = 1.0
    to = lambda a: jnp.asarray(a, BF16)
    return to(tri), to(tri.T), to(e128), to(e64)


def _pad_lanes(v, width=LANES):
    return jnp.pad(v, ((0, 0), (0, width - v.shape[1])))


def _prepare(rel_bias, w_in, conv_w, conv_b, dt_bias_fwd, dt_bias_bwd, a_log_fwd, a_log_bwd, d_skip,
             attn_norm_g, ssd_norm_g, w_out, ln1_g, ln1_b, w_gate, w_up, w_down, ln2_g, ln2_b):
    w = w_in[0]
    o = 3 * D_ATTN
    p = {}
    p["wqkv"] = w[:, :o].astype(BF16)
    p["wz"] = w[:, o:o + D_SSD].astype(BF16)
    p["wxbc"] = w[:, o + D_SSD:o + D_SSD + D_XBC].astype(BF16)
    wdt = w[:, o + D_SSD + D_XBC:].astype(BF16)
    p["wdt"] = _pad_lanes(wdt)
    p["wdtt"] = wdt.T
    dtb = jnp.concatenate([dt_bias_fwd[0], dt_bias_bwd[0]]).astype(F32)[None, :]
    p["dtb"] = _pad_lanes(dtb)
    p["dtbt"] = jnp.broadcast_to(dtb.T, (2 * N_HEADS, LANES))
    alog = jnp.concatenate([a_log_fwd[0], a_log_bwd[0]]).astype(F32)[None, :]
    tri, tri_t, e128, e64 = _selector_constants()
    convw = jnp.pad(conv_w[0].astype(F32), ((0, HALO - SSD_CONV), (0, 0)))
    base = (convw, conv_b[0].astype(F32)[None, :], _pad_lanes(alog), jnp.broadcast_to(alog.T, (2 * N_HEADS, LANES)))
    p["ssd_fwd"] = base + (tri, tri_t, e128[0], e64[0])
    p["ssd_bwd"] = base + (tri_t, tri, e128[1], e64[1])
    p["dskip"] = jnp.repeat(d_skip[0].astype(F32), HEAD_DIM)[None, :]
    p["ssd_g"] = ssd_norm_g[0].astype(F32)[None, :]
    p["attn_g"] = attn_norm_g[0].astype(F32)[None, :]
    p["bias"] = _bias_tables(rel_bias)
    p["e64_attn"] = e64[0]
    wo = w_out[0].astype(BF16)
    p["ffn"] = (wo[:D_ATTN], wo[D_ATTN:], ln1_g[0].astype(F32)[None, :], ln1_b[0].astype(F32)[None, :],
                w_gate[0].astype(BF16), w_up[0].astype(BF16), w_down[0].astype(BF16),
                ln2_g[0].astype(F32)[None, :], ln2_b[0].astype(F32)[None, :])
    return p


def _encoder_layer(x, p):
    nseq, seq, _ = x.shape
    (q1, k1, v1, q4, k4, v4, q16, k16, v16, z, xbc, dt, dtt) = _in_proj(
        x, p["wqkv"], p["wz"], p["wxbc"], p["wdt"], p["wdtt"], p["dtb"], p["dtbt"])

    y_fwd = _ssd_pass(xbc, dt, dtt, p["ssd_fwd"], None, backward=False)
    y_ssd = _ssd_pass(xbc, dt, dtt, p["ssd_bwd"], (y_fwd, z, p["dskip"], p["ssd_g"]), backward=True)

    outs, lses = [], []
    for d, (q, k, v), bias in zip(DILATIONS, ((q1, k1, v1), (q4, k4, v4), (q16, k16, v16)), p["bias"]):
        flat = lambda t: t.reshape(nseq * d, seq // d, D_ATTN)
        o, lse = _attn_branch(flat(q), flat(k), flat(v), bias)
        if d > 1:
            o = o.reshape(nseq, d, seq // d, D_ATTN)
            lse = lse.reshape(nseq, d, seq // d, LANES)
        outs.append(o)
        lses.append(lse)
    attn = _merge(*outs, *lses, p["e64_attn"], p["attn_g"])
    return _out_ffn(x, attn, y_ssd, *p["ffn"])


def kernel(x_prompt, x_sample, rel_bias, w_in, conv_w, conv_b, dt_bias_fwd, dt_bias_bwd, a_log_fwd,
           a_log_bwd, d_skip, attn_norm_g, ssd_norm_g, w_out, ln1_g, ln1_b, w_gate, w_up, w_down,
           ln2_g, ln2_b):
    p = _prepare(rel_bias, w_in, conv_w, conv_b, dt_bias_fwd, dt_bias_bwd, a_log_fwd, a_log_bwd, d_skip,
                 attn_norm_g, ssd_norm_g, w_out, ln1_g, ln1_b, w_gate, w_up, w_down, ln2_g, ln2_b)
    return (_encoder_layer(x_prompt, p), _encoder_layer(x_sample, p))
```

```python
import functools
import math

import jax
import jax.numpy as jnp
import numpy as np
from jax import lax
from jax.experimental import pallas as pl
from jax.experimental.pallas import tpu as pltpu

F32 = jnp.float32
BF16 = jnp.bfloat16

D_MODEL = 1024
D_ATTN = 512
HEAD_DIM = 64
N_HEADS = 8
D_SSD = 512
SSD_GROUPS = 2
SSD_STATE = 128
SSD_CONV = 5
CHUNK = 128
D_FF = 2816
D_XBC = D_SSD + 2 * SSD_GROUPS * SSD_STATE
DILATIONS = (1, 4, 16)
SIDE = 64
REL_BUCKETS = 32
REL_MAX_DIST = 1024
DEPTH = 1
ALPHA = (2 * DEPTH) ** 0.25
EPS = 1e-5
NEG = -1e30

LANES = 128
HALO = 8
VMEM_LIMIT = 56 * 1024 * 1024

TM_PROJ = 512
R_SSD = 512
RB_ATT = 512
QB = 128
TC_MERGE = 512
TM_FFN = 512
FF_CHUNK = 1408


def _const_spec(shape):
    nd = len(shape)
    return pl.BlockSpec(shape, lambda *_: (0,) * nd, pipeline_mode=pl.Buffered(1))


def _params(n_axes):
    return pltpu.CompilerParams(dimension_semantics=("arbitrary",) * n_axes,
                                vmem_limit_bytes=VMEM_LIMIT)


def _sigmoid(x):
    return 1.0 / (1.0 + jnp.exp(-x))


def _softplus(x):
    return jnp.maximum(x, 0.0) + jnp.log1p(jnp.exp(-jnp.abs(x)))


def _split3(x):
    hi = x.astype(BF16)
    r1 = x - hi.astype(F32)
    mid = r1.astype(BF16)
    lo = (r1 - mid.astype(F32)).astype(BF16)
    return hi, mid, lo


def _exact_dot_r(sel, x):
    hi, mid, lo = _split3(x)
    d = functools.partial(jnp.dot, preferred_element_type=F32)
    return d(sel, hi) + d(sel, mid) + d(sel, lo)


def _exact_dot_l(x, sel):
    hi, mid, lo = _split3(x)
    d = functools.partial(jnp.dot, preferred_element_type=F32)
    return d(hi, sel) + d(mid, sel) + d(lo, sel)


def _lane_lt64(shape):
    return lax.broadcasted_iota(jnp.int32, shape, len(shape) - 1) < HEAD_DIM


def _in_proj_kernel(x_ref, wqkv_ref, wz_ref, wxbc_ref, wdt_ref, wdtt_ref, dtb_ref, dtbt_ref,
                    q1_ref, k1_ref, v1_ref, q4_ref, k4_ref, v4_ref, q16_ref, k16_ref, v16_ref,
                    z_ref, xbc_ref, dt_ref, dtt_ref, slab_ref, slab4_ref, *, tm):
    xb = x_ref[...].astype(BF16)
    dot = functools.partial(jnp.dot, preferred_element_type=F32)

    qkv = dot(xb, wqkv_ref[...])
    n_slab = 3 * D_ATTN // LANES
    per = D_ATTN // LANES
    for s in range(n_slab):
        col = qkv[:, s * LANES:(s + 1) * LANES]
        if s < per:
            col = col * (HEAD_DIM ** -0.5)
        slab_ref[s] = col

    z_ref[...] = dot(xb, wz_ref[...])
    xbc_ref[...] = dot(xb, wxbc_ref[...])
    dt_ref[...] = _softplus(dot(xb, wdt_ref[...]) + dtb_ref[...])
    dtt = lax.dot_general(wdtt_ref[...], xb, (((1,), (1,)), ((), ())), preferred_element_type=F32)
    dtt = _softplus(dtt + dtbt_ref[:, 0:1])
    for c in range(tm // CHUNK):
        dtt_ref[c] = dtt[:, c * CHUNK:(c + 1) * CHUNK]

    nat = (q1_ref, k1_ref, v1_ref)
    d4 = (q4_ref, k4_ref, v4_ref)
    d16 = (q16_ref, k16_ref, v16_ref)
    n4, n16 = tm // 4, tm // 16
    for s in range(n_slab):
        a, c = divmod(s, per)
        lanes = slice(c * LANES, (c + 1) * LANES)
        nat[a][:, lanes] = slab_ref[s].astype(BF16)
        for r in range(4):
            part = slab_ref[s, pl.ds(r, n4, stride=4), :]
            d4[a][r, :, lanes] = part.astype(BF16)
            slab4_ref[s, r * n4:(r + 1) * n4, :] = part
        for r in range(4):
            for hi in range(4):
                part = slab4_ref[s, pl.ds(r * n4 + hi, n16, stride=4), :]
                d16[a][r + 4 * hi, :, lanes] = part.astype(BF16)


def _in_proj(x, wqkv, wz, wxbc, wdt, wdtt, dtb, dtbt):
    nseq, seq, _ = x.shape
    tm = min(TM_PROJ, seq)
    grid = (nseq, seq // tm)
    row = lambda w: pl.BlockSpec((None, tm, w), lambda n, j: (n, j, 0))
    res = lambda d: pl.BlockSpec((None, d, tm // d, D_ATTN), lambda n, j: (n, 0, j, 0))
    sds = jax.ShapeDtypeStruct
    out_shape = (
        [sds((nseq, seq, D_ATTN), BF16)] * 3
        + [sds((nseq, 4, seq // 4, D_ATTN), BF16)] * 3
        + [sds((nseq, 16, seq // 16, D_ATTN), BF16)] * 3
        + [sds((nseq, seq, D_SSD), F32), sds((nseq, seq, D_XBC), F32), sds((nseq, seq, LANES), F32),
           sds((nseq, seq // CHUNK, 16, CHUNK), F32)]
    )
    out_specs = (
        [row(D_ATTN)] * 3 + [res(4)] * 3 + [res(16)] * 3
        + [row(D_SSD), row(D_XBC), row(LANES),
           pl.BlockSpec((None, tm // CHUNK, 16, CHUNK), lambda n, j: (n, j, 0, 0))]
    )
    in_specs = [row(D_MODEL)] + [_const_spec(w.shape) for w in (wqkv, wz, wxbc, wdt, wdtt, dtb, dtbt)]
    return pl.pallas_call(
        functools.partial(_in_proj_kernel, tm=tm),
        grid=grid, in_specs=in_specs, out_specs=out_specs, out_shape=out_shape,
        scratch_shapes=[pltpu.VMEM((3 * D_ATTN // LANES, tm, LANES), F32)] * 2,
        compiler_params=_params(2), name="in_proj",
    )(x, wqkv, wz, wxbc, wdt, wdtt, dtb, dtbt)


def _conv_silu_chunk(ext_ref, convw_ref, convb_ref, c):
    span = CHUNK + 2 * HALO
    halves = []
    for half in range(D_XBC // 512):
        lanes = slice(half * 512, (half + 1) * 512)
        xe = ext_ref[c * CHUNK:c * CHUNK + span, lanes]
        acc = jnp.broadcast_to(convb_ref[:, lanes], (span, 512))
        for k in range(SSD_CONV):
            shift = (SSD_CONV // 2 - k) % span
            xk = xe if shift == 0 else pltpu.roll(xe, shift, 0)
            acc = acc + convw_ref[k:k + 1, lanes] * xk
        acc = acc[HALO:HALO + CHUNK]
        halves.append(acc * _sigmoid(acc))
    return jnp.concatenate(halves, axis=1)


def _ssd_chunk(xs, bm32, cm32, dt, dtt, state_ref, a_row, a_col, tcol, trow, *, backward):
    head0 = N_HEADS if backward else 0
    tot_row = 0 if backward else CHUNK - 1
    qi = lax.broadcasted_iota(jnp.int32, (CHUNK, CHUNK), 0)
    si = lax.broadcasted_iota(jnp.int32, (CHUNK, CHUNK), 1)
    causal = (si >= qi) if backward else (si <= qi)
    lt64 = _lane_lt64((CHUNK, LANES))
    dot = functools.partial(jnp.dot, preferred_element_type=F32)
    nt = (((1,), (1,)), ((), ()))

    acum = _exact_dot_r(tcol, dt * a_row)
    acum_t = _exact_dot_l(dtt * a_col, trow)
    bm = [b.astype(BF16) for b in bm32]
    cb = [lax.dot_general(cm32[g].astype(BF16), bm[g], nt, preferred_element_type=F32)
          for g in range(SSD_GROUPS)]

    y_off = [dot(cm32[g].astype(BF16), state_ref[g].astype(BF16)) for g in range(SSD_GROUPS)]

    y_tiles, xw_tiles, cd_tiles = [], [], []
    for p in range(N_HEADS // 2):
        g = p // 2
        cols = []
        for h in (2 * p, 2 * p + 1):
            lane = head0 + h
            cols.append((jnp.broadcast_to(acum[:, lane:lane + 1], (CHUNK, LANES)),
                         jnp.broadcast_to(dt[:, lane:lane + 1], (CHUNK, LANES))))
        a64 = jnp.where(lt64, cols[0][0], cols[1][0])
        dt64 = jnp.where(lt64, cols[0][1], cols[1][1])
        tot = a64[tot_row:tot_row + 1, :]
        xdt = xs[:, p * LANES:(p + 1) * LANES] * dt64
        xw_tiles.append((xdt * jnp.exp(tot - a64)).astype(BF16))
        cd_tiles.append(jnp.exp(tot))
        xdt_b = xdt.astype(BF16)
        ys = []
        for i, h in enumerate((2 * p, 2 * p + 1)):
            arow = jnp.broadcast_to(acum_t[head0 + h:head0 + h + 1, :], (CHUNK, CHUNK))
            decay = jnp.exp(jnp.where(causal, cols[i][0] - arow, -jnp.inf))
            ys.append(dot((cb[g] * decay).astype(BF16), xdt_b))
        off = y_off[g][:, (p % 2) * LANES:(p % 2 + 1) * LANES]
        y_tiles.append(jnp.where(lt64, ys[0], ys[1]) + off * jnp.exp(a64))
    for g in range(SSD_GROUPS):
        xw = jnp.concatenate(xw_tiles[2 * g:2 * g + 2], axis=1)
        cd = jnp.concatenate(cd_tiles[2 * g:2 * g + 2], axis=1)
        upd = lax.dot_general(bm[g], xw, (((0,), (0,)), ((), ())), preferred_element_type=F32)
        state_ref[g] = cd * state_ref[g] + upd
    return jnp.concatenate(y_tiles, axis=1)


def _ssd_fwd_kernel(xprev_ref, xcur_ref, xnext_ref, dt_ref, dtt_ref, convw_ref, convb_ref,
                    alog_ref, alogt_ref, tcol_ref, trow_ref, dskip_ref,
                    act_ref, yf_ref, ext_ref, state_ref, *, rows, nblk):
    j = pl.program_id(1)

    @pl.when(j == 0)
    def _():
        state_ref[...] = jnp.zeros_like(state_ref)

    keep_prev = jnp.where(j == 0, 0.0, 1.0)
    keep_next = jnp.where(j == nblk - 1, 0.0, 1.0)
    ext_ref[0:HALO, :] = xprev_ref[...] * keep_prev
    ext_ref[HALO:HALO + rows, :] = xcur_ref[...]
    ext_ref[HALO + rows:2 * HALO + rows, :] = xnext_ref[...] * keep_next

    a_row = -jnp.exp(alog_ref[...])
    a_col = -jnp.exp(alogt_ref[:, 0:1])
    for c in range(rows // CHUNK):
        rs = slice(c * CHUNK, (c + 1) * CHUNK)
        act = _conv_silu_chunk(ext_ref, convw_ref, convb_ref, c)
        act_ref[rs, :] = act.astype(act_ref.dtype)
        xs = act[:, 0:D_SSD]
        bm = [act[:, D_SSD + g * SSD_STATE:D_SSD + (g + 1) * SSD_STATE] for g in range(SSD_GROUPS)]
        cm = [act[:, D_SSD + (SSD_GROUPS + g) * SSD_STATE:D_SSD + (SSD_GROUPS + g + 1) * SSD_STATE]
              for g in range(SSD_GROUPS)]
        y = _ssd_chunk(xs, bm, cm, dt_ref[rs, :], dtt_ref[c], state_ref, a_row, a_col,
                       tcol_ref[...], trow_ref[...], backward=False)
        yf_ref[rs, :] = y + dskip_ref[...] * xs


def _ssd_bwd_kernel(act_ref, dt_ref, dtt_ref, alog_ref, alogt_ref, tcol_ref, trow_ref,
                    yf_ref, z_ref, gnorm_ref, out_ref, state_ref, *, rows):
    j = pl.program_id(1)

    @pl.when(j == 0)
    def _():
        state_ref[...] = jnp.zeros_like(state_ref)

    a_row = -jnp.exp(alog_ref[...])
    a_col = -jnp.exp(alogt_ref[:, 0:1])
    for c in reversed(range(rows // CHUNK)):
        rs = slice(c * CHUNK, (c + 1) * CHUNK)
        xs = act_ref[rs, 0:D_SSD].astype(F32)
        bm = [act_ref[rs, D_SSD + g * SSD_STATE:D_SSD + (g + 1) * SSD_STATE].astype(F32)
              for g in range(SSD_GROUPS)]
        cm = [act_ref[rs, D_SSD + (SSD_GROUPS + g) * SSD_STATE:D_SSD + (SSD_GROUPS + g + 1) * SSD_STATE]
              .astype(F32) for g in range(SSD_GROUPS)]
        y = _ssd_chunk(xs, bm, cm, dt_ref[rs, :], dtt_ref[c], state_ref, a_row, a_col,
                       tcol_ref[...], trow_ref[...], backward=True)
        zz = z_ref[rs, :]
        tot = (yf_ref[rs, :] + y) * (zz * _sigmoid(zz))
        inv = lax.rsqrt(jnp.mean(tot * tot, axis=-1, keepdims=True) + EPS)
        out_ref[rs, :] = (tot * inv * gnorm_ref[...]).astype(out_ref.dtype)


def _ssd(xbc, dt, dtt, z, consts_fwd, consts_bwd, dskip, gnorm):
    nseq, seq, _ = xbc.shape
    rows = min(R_SSD, seq)
    nblk = seq // rows
    hb = rows // HALO
    nh = seq // HALO
    state = pltpu.VMEM((SSD_GROUPS, SSD_STATE, 2 * LANES), F32)

    def specs(blk):
        cur = lambda w: pl.BlockSpec((None, rows, w), lambda n, j: (n, blk(j), 0))
        dtt_spec = pl.BlockSpec((None, rows // CHUNK, 16, CHUNK), lambda n, j: (n, blk(j), 0, 0))
        return cur, dtt_spec

    cur, dtt_spec = specs(lambda j: j)
    act, y_fwd = pl.pallas_call(
        functools.partial(_ssd_fwd_kernel, rows=rows, nblk=nblk),
        grid=(nseq, nblk),
        in_specs=[pl.BlockSpec((None, HALO, D_XBC), lambda n, j: (n, jnp.maximum(j * hb - 1, 0), 0)),
                  cur(D_XBC),
                  pl.BlockSpec((None, HALO, D_XBC), lambda n, j: (n, jnp.minimum((j + 1) * hb, nh - 1), 0)),
                  cur(LANES), dtt_spec]
        + [_const_spec(c.shape) for c in consts_fwd] + [_const_spec(dskip.shape)],
        out_specs=[cur(D_XBC), cur(D_SSD)],
        out_shape=[jax.ShapeDtypeStruct((nseq, seq, D_XBC), BF16),
                   jax.ShapeDtypeStruct((nseq, seq, D_SSD), F32)],
        scratch_shapes=[pltpu.VMEM((rows + 2 * HALO, D_XBC), F32), state],
        compiler_params=_params(2), name="ssd_fwd",
    )(xbc, xbc, xbc, dt, dtt, *consts_fwd, dskip)

    cur, dtt_spec = specs(lambda j: nblk - 1 - j)
    return pl.pallas_call(
        functools.partial(_ssd_bwd_kernel, rows=rows),
        grid=(nseq, nblk),
        in_specs=[cur(D_XBC), cur(LANES), dtt_spec] + [_const_spec(c.shape) for c in consts_bwd]
        + [cur(D_SSD), cur(D_SSD), _const_spec(gnorm.shape)],
        out_specs=cur(D_SSD),
        out_shape=jax.ShapeDtypeStruct((nseq, seq, D_SSD), BF16),
        scratch_shapes=[state],
        compiler_params=_params(2), name="ssd_bwd",
    )(act, dt, dtt, *consts_bwd, y_fwd, z, gnorm)


def _attn_kernel(q_ref, kprev_ref, kcur_ref, knext_ref, vprev_ref, vcur_ref, vnext_ref, bias_ref,
                 o_ref, lse_ref, kbuf_ref, vbuf_ref, *, rows, nblk):
    j = pl.program_id(1)
    first = j == 0
    last = j == nblk - 1
    for buf, prev, cur, nxt in ((kbuf_ref, kprev_ref, kcur_ref, knext_ref),
                                (vbuf_ref, vprev_ref, vcur_ref, vnext_ref)):
        buf[0:SIDE, :] = prev[...]
        buf[SIDE:SIDE + rows, :] = cur[...]
        buf[SIDE + rows:2 * SIDE + rows, :] = nxt[...]
    win = QB + 2 * SIDE
    lt64 = _lane_lt64((QB, LANES))
    lane = lax.broadcasted_iota(jnp.int32, (QB, LANES), 1)
    col = lax.broadcasted_iota(jnp.int32, (2 * QB, win), 1)
    nqb = rows // QB
    for qb in range(nqb):
        qrows = slice(qb * QB, (qb + 1) * QB)
        wrows = slice(qb * QB, qb * QB + win)
        lse_pk = jnp.zeros((QB, LANES), F32)
        for p in range(N_HEADS // 2):
            lanes = slice(p * LANES, (p + 1) * LANES)
            qp = q_ref[qrows, lanes]
            zero = jnp.zeros_like(qp)
            q2 = jnp.concatenate([jnp.where(lt64, qp, zero), jnp.where(lt64, zero, qp)], axis=0)
            s = lax.dot_general(q2, kbuf_ref[wrows, lanes], (((1,), (1,)), ((), ())),
                                preferred_element_type=F32) + bias_ref[p]
            if qb == 0:
                s = jnp.where(jnp.logical_and(col < SIDE, first), NEG, s)
            if qb == nqb - 1:
                s = jnp.where(jnp.logical_and(col >= win - SIDE, last), NEG, s)
            m = jnp.max(s, axis=1, keepdims=True)
            e = jnp.exp(s - m)
            l = jnp.sum(e, axis=1, keepdims=True)
            o2 = jnp.dot(e.astype(BF16), vbuf_ref[wrows, lanes], preferred_element_type=F32)
            o2 = o2 * (1.0 / l)
            o_ref[qrows, lanes] = jnp.where(lt64, o2[0:QB], o2[QB:2 * QB])
            lse2 = m + jnp.log(l)
            lse_pk = (lse_pk + jnp.where(lane == 2 * p, lse2[0:QB], 0.0)
                      + jnp.where(lane == 2 * p + 1, lse2[QB:2 * QB], 0.0))
        lse_ref[qrows, :] = lse_pk


def _attn_branch(q, k, v, bias):
    ng, length, _ = q.shape
    rows = min(RB_ATT, length)
    nblk = length // rows
    hb = rows // SIDE
    nh = length // SIDE
    cur = lambda w: pl.BlockSpec((None, rows, w), lambda g, j: (g, j, 0))
    prev = pl.BlockSpec((None, SIDE, D_ATTN), lambda g, j: (g, jnp.maximum(j * hb - 1, 0), 0))
    nxt = pl.BlockSpec((None, SIDE, D_ATTN), lambda g, j: (g, jnp.minimum((j + 1) * hb, nh - 1), 0))
    return pl.pallas_call(
        functools.partial(_attn_kernel, rows=rows, nblk=nblk),
        grid=(ng, nblk),
        in_specs=[cur(D_ATTN), prev, cur(D_ATTN), nxt, prev, cur(D_ATTN), nxt, _const_spec(bias.shape)],
        out_specs=[cur(D_ATTN), cur(LANES)],
        out_shape=[jax.ShapeDtypeStruct((ng, length, D_ATTN), F32),
                   jax.ShapeDtypeStruct((ng, length, LANES), F32)],
        scratch_shapes=[pltpu.VMEM((rows + 2 * SIDE, D_ATTN), BF16),
                        pltpu.VMEM((rows + 2 * SIDE, D_ATTN), BF16)],
        compiler_params=_params(2), name="dilated_attn",
    )(q, k, k, k, v, v, v, bias)


def _merge_kernel(o1_ref, o4_ref, o16_ref, l1_ref, l4_ref, l16_ref, e64_ref, gnorm_ref, out_ref,
                  so4_ref, so16_ref, sl4_ref, sl16_ref, *, tc):
    per = D_ATTN // LANES
    for d, o_ref, l_ref, so_ref, sl_ref in ((4, o4_ref, l4_ref, so4_ref, sl4_ref),
                                            (16, o16_ref, l16_ref, so16_ref, sl16_ref)):
        for r in range(d):
            rows = pl.ds(r, tc // d, stride=d)
            sl_ref[rows, :] = l_ref[r]
            for s in range(per):
                so_ref[s, rows, :] = o_ref[r, :, s * LANES:(s + 1) * LANES]
    for b in range(tc // QB):
        rs = slice(b * QB, (b + 1) * QB)
        l1, l4, l16 = l1_ref[rs, :], sl4_ref[rs, :], sl16_ref[rs, :]
        m = jnp.maximum(jnp.maximum(l1, l4), l16)
        e1, e4, e16 = jnp.exp(l1 - m), jnp.exp(l4 - m), jnp.exp(l16 - m)
        inv = 1.0 / (e1 + e4 + e16)
        w1 = _exact_dot_l(e1 * inv, e64_ref[...])
        w4 = _exact_dot_l(e4 * inv, e64_ref[...])
        w16 = _exact_dot_l(e16 * inv, e64_ref[...])
        o4 = jnp.concatenate([so4_ref[s, rs, :] for s in range(per)], axis=1)
        o16 = jnp.concatenate([so16_ref[s, rs, :] for s in range(per)], axis=1)
        o = w1 * o1_ref[rs, :] + w4 * o4 + w16 * o16
        inv_rms = lax.rsqrt(jnp.mean(o * o, axis=-1, keepdims=True) + EPS)
        out_ref[rs, :] = (o * inv_rms * gnorm_ref[...]).astype(out_ref.dtype)


def _merge(o1, o4, o16, l1, l4, l16, e64, gnorm):
    nseq, seq, _ = o1.shape
    tc = min(TC_MERGE, seq)
    row = lambda w: pl.BlockSpec((None, tc, w), lambda n, j: (n, j, 0))
    res = lambda d, w: pl.BlockSpec((None, d, tc // d, w), lambda n, j: (n, 0, j, 0))
    return pl.pallas_call(
        functools.partial(_merge_kernel, tc=tc),
        grid=(nseq, seq // tc),
        in_specs=[row(D_ATTN), res(4, D_ATTN), res(16, D_ATTN), row(LANES), res(4, LANES), res(16, LANES),
                  _const_spec(e64.shape), _const_spec(gnorm.shape)],
        out_specs=row(D_ATTN),
        out_shape=jax.ShapeDtypeStruct((nseq, seq, D_ATTN), BF16),
        scratch_shapes=[pltpu.VMEM((D_ATTN // LANES, tc, LANES), F32),
                        pltpu.VMEM((D_ATTN // LANES, tc, LANES), F32),
                        pltpu.VMEM((tc, LANES), F32), pltpu.VMEM((tc, LANES), F32)],
        compiler_params=_params(2), name="merge",
    )(o1, o4, o16, l1, l4, l16, e64, gnorm)


def _layer_norm(x, g, b):
    mu = jnp.mean(x, axis=-1, keepdims=True)
    xc = x - mu
    var = jnp.mean(xc * xc, axis=-1, keepdims=True)
    return xc * lax.rsqrt(var + EPS) * g + b


def _ffn_kernel(x_ref, a_ref, y_ref, woa_ref, woy_ref, g1_ref, b1_ref, wg_ref, wu_ref, wd_ref,
                g2_ref, b2_ref, out_ref):
    dot = functools.partial(jnp.dot, preferred_element_type=F32)
    mix = dot(a_ref[...], woa_ref[...]) + dot(y_ref[...], woy_ref[...])
    h1 = _layer_norm(ALPHA * x_ref[...] + mix, g1_ref[...], b1_ref[...])
    h1b = h1.astype(BF16)
    ffn = jnp.zeros_like(h1)
    for c in range(D_FF // FF_CHUNK):
        cols = slice(c * FF_CHUNK, (c + 1) * FF_CHUNK)
        gate = dot(h1b, wg_ref[:, cols])
        up = dot(h1b, wu_ref[:, cols])
        hid = (gate * _sigmoid(gate) * up).astype(BF16)
        ffn = ffn + dot(hid, wd_ref[cols, :])
    out_ref[...] = _layer_norm(ALPHA * h1 + ffn, g2_ref[...], b2_ref[...])


def _out_ffn(x, attn, y, woa, woy, g1, b1, wg, wu, wd, g2, b2):
    nseq, seq, _ = x.shape
    tm = min(TM_FFN, seq)
    row = lambda w: pl.BlockSpec((None, tm, w), lambda n, j: (n, j, 0))
    consts = (woa, woy, g1, b1, wg, wu, wd, g2, b2)
    return pl.pallas_call(
        _ffn_kernel,
        grid=(nseq, seq // tm),
        in_specs=[row(D_MODEL), row(D_ATTN), row(D_SSD)] + [_const_spec(c.shape) for c in consts],
        out_specs=row(D_MODEL),
        out_shape=jax.ShapeDtypeStruct((nseq, seq, D_MODEL), F32),
        compiler_params=_params(2), name="out_ffn",
    )(x, attn, y, *consts)


def _t5_bucket(rel):
    half = REL_BUCKETS // 2
    max_exact = half // 2
    ret = (rel > 0).astype(np.int32) * half
    n = np.abs(rel)
    large = max_exact + (np.log(np.maximum(n, 1) / max_exact)
                         / math.log(REL_MAX_DIST / max_exact) * (half - max_exact)).astype(np.int32)
    large = np.minimum(large, half - 1)
    return ret + np.where(n < max_exact, n, large)


def _bias_tables(rel_bias):
    win = QB + 2 * SIDE
    period = win + QB
    tables = []
    for d in DILATIONS:
        buckets = _t5_bucket((np.arange(2 * SIDE + 1) - SIDE) * d)
        band = rel_bias.astype(F32)[buckets].T
        vec = jnp.pad(band, ((0, 0), (0, period - band.shape[1])), constant_values=NEG)
        flat = jnp.tile(vec, (1, QB))[:, :QB * (period - 1)]
        toe = flat.reshape(N_HEADS, QB, period - 1)[:, :, :win]
        tables.append(toe.reshape(N_HEADS // 2, 2 * QB, win))
    return tables


def _selector_constants():
    tri = np.tril(np.ones((CHUNK, CHUNK), np.float32))
    e64 = np.zeros((LANES, D_ATTN), np.float32)
    for h in range(N_HEADS):
        e64[h, h * HEAD_DIM:(h + 1) * HEAD_DIM] = 1.0
    to = lambda a: jnp.asarray(a, BF16)
    return to(tri), to(tri.T), to(e64)


def _pad_lanes(v, width=LANES):
    return jnp.pad(v, ((0, 0), (0, width - v.shape[1])))


def _prepare(rel_bias, w_in, conv_w, conv_b, dt_bias_fwd, dt_bias_bwd, a_log_fwd, a_log_bwd, d_skip,
             attn_norm_g, ssd_norm_g, w_out, ln1_g, ln1_b, w_gate, w_up, w_down, ln2_g, ln2_b):
    w = w_in[0]
    o = 3 * D_ATTN
    p = {}
    p["wqkv"] = w[:, :o].astype(BF16)
    p["wz"] = w[:, o:o + D_SSD].astype(BF16)
    p["wxbc"] = w[:, o + D_SSD:o + D_SSD + D_XBC].astype(BF16)
    wdt = w[:, o + D_SSD + D_XBC:].astype(BF16)
    p["wdt"] = _pad_lanes(wdt)
    p["wdtt"] = wdt.T
    dtb = jnp.concatenate([dt_bias_fwd[0], dt_bias_bwd[0]]).astype(F32)[None, :]
    p["dtb"] = _pad_lanes(dtb)
    p["dtbt"] = jnp.broadcast_to(dtb.T, (2 * N_HEADS, LANES))
    alog = jnp.concatenate([a_log_fwd[0], a_log_bwd[0]]).astype(F32)[None, :]
    tri, tri_t, e64 = _selector_constants()
    convw = jnp.pad(conv_w[0].astype(F32), ((0, HALO - SSD_CONV), (0, 0)))
    a_tabs = (_pad_lanes(alog), jnp.broadcast_to(alog.T, (2 * N_HEADS, LANES)))
    p["ssd_fwd"] = (convw, conv_b[0].astype(F32)[None, :]) + a_tabs + (tri, tri_t)
    p["ssd_bwd"] = a_tabs + (tri_t, tri)
    p["dskip"] = jnp.repeat(d_skip[0].astype(F32), HEAD_DIM)[None, :]
    p["ssd_g"] = ssd_norm_g[0].astype(F32)[None, :]
    p["attn_g"] = attn_norm_g[0].astype(F32)[None, :]
    p["bias"] = _bias_tables(rel_bias)
    p["e64_attn"] = e64
    wo = w_out[0].astype(BF16)
    p["ffn"] = (wo[:D_ATTN], wo[D_ATTN:], ln1_g[0].astype(F32)[None, :], ln1_b[0].astype(F32)[None, :],
                w_gate[0].astype(BF16), w_up[0].astype(BF16), w_down[0].astype(BF16),
                ln2_g[0].astype(F32)[None, :], ln2_b[0].astype(F32)[None, :])
    return p


def _encoder_layer(x, p):
    nseq, seq, _ = x.shape
    (q1, k1, v1, q4, k4, v4, q16, k16, v16, z, xbc, dt, dtt) = _in_proj(
        x, p["wqkv"], p["wz"], p["wxbc"], p["wdt"], p["wdtt"], p["dtb"], p["dtbt"])

    y_ssd = _ssd(xbc, dt, dtt, z, p["ssd_fwd"], p["ssd_bwd"], p["dskip"], p["ssd_g"])

    outs, lses = [], []
    for d, (q, k, v), bias in zip(DILATIONS, ((q1, k1, v1), (q4, k4, v4), (q16, k16, v16)), p["bias"]):
        flat = lambda t: t.reshape(nseq * d, seq // d, D_ATTN)
        o, lse = _attn_branch(flat(q), flat(k), flat(v), bias)
        if d > 1:
            o = o.reshape(nseq, d, seq // d, D_ATTN)
            lse = lse.reshape(nseq, d, seq // d, LANES)
        outs.append(o)
        lses.append(lse)
    attn = _merge(*outs, *lses, p["e64_attn"], p["attn_g"])
    return _out_ffn(x, attn, y_ssd, *p["ffn"])


def kernel(x_prompt, x_sample, rel_bias, w_in, conv_w, conv_b, dt_bias_fwd, dt_bias_bwd, a_log_fwd,
           a_log_bwd, d_skip, attn_norm_g, ssd_norm_g, w_out, ln1_g, ln1_b, w_gate, w_up, w_down,
           ln2_g, ln2_b):
    p = _prepare(rel_bias, w_in, conv_w, conv_b, dt_bias_fwd, dt_bias_bwd, a_log_fwd, a_log_bwd, d_skip,
                 attn_norm_g, ssd_norm_g, w_out, ln1_g, ln1_b, w_gate, w_up, w_down, ln2_g, ln2_b)
    return (_encoder_layer(x_prompt, p), _encoder_layer(x_sample, p))
```

```python
import functools
import math

import jax
import jax.numpy as jnp
import numpy as np
from jax import lax
from jax.experimental import pallas as pl
from jax.experimental.pallas import tpu as pltpu

F32 = jnp.float32
BF16 = jnp.bfloat16

D_MODEL = 1024
D_ATTN = 512
HEAD_DIM = 64
N_HEADS = 8
D_SSD = 512
SSD_GROUPS = 2
SSD_STATE = 128
SSD_CONV = 5
CHUNK = 128
D_FF = 2816
D_XBC = D_SSD + 2 * SSD_GROUPS * SSD_STATE
DILATIONS = (1, 4, 16)
SIDE = 64
REL_BUCKETS = 32
REL_MAX_DIST = 1024
DEPTH = 1
ALPHA = (2 * DEPTH) ** 0.25
EPS = 1e-5
NEG = -1e30
LOG2E = 1.4426950408889634

LANES = 128
HALO = 8
VMEM_LIMIT = 56 * 1024 * 1024

TM_PROJ = 512
R_SSD = 512
RB_ATT = 512
QB = 128
ATT_PIPE = 3
TC_MERGE = 512
TM_FFN = 512
FF_CHUNK = 256


def _const_spec(shape):
    nd = len(shape)
    return pl.BlockSpec(shape, lambda *_: (0,) * nd, pipeline_mode=pl.Buffered(1))


def _params(n_axes):
    return pltpu.CompilerParams(dimension_semantics=("arbitrary",) * n_axes,
                                vmem_limit_bytes=VMEM_LIMIT)


def _sigmoid(x):
    return 1.0 / (1.0 + jnp.exp(-x))


def _softplus(x):
    return jnp.maximum(x, 0.0) + jnp.log1p(jnp.exp(-jnp.abs(x)))


def _split3(x):
    hi = x.astype(BF16)
    r1 = x - hi.astype(F32)
    mid = r1.astype(BF16)
    lo = (r1 - mid.astype(F32)).astype(BF16)
    return hi, mid, lo


def _exact_dot_r(sel, x):
    hi, mid, lo = _split3(x)
    d = functools.partial(jnp.dot, preferred_element_type=F32)
    return d(sel, hi) + d(sel, mid) + d(sel, lo)


def _exact_dot_l(x, sel):
    hi, mid, lo = _split3(x)
    d = functools.partial(jnp.dot, preferred_element_type=F32)
    return d(hi, sel) + d(mid, sel) + d(lo, sel)


def _lane_lt64(shape):
    return lax.broadcasted_iota(jnp.int32, shape, len(shape) - 1) < HEAD_DIM


def _lse_lane(h):
    return h if h % 2 == 0 else HEAD_DIM + h


def _in_proj_kernel(x_ref, wqkv_ref, wz_ref, wxbc_ref, wdt_ref, wdtt_ref, dtb_ref, dtbt_ref,
                    q1_ref, k1_ref, v1_ref, q4_ref, k4_ref, v4_ref, q16_ref, k16_ref, v16_ref,
                    z_ref, xbc_ref, dt_ref, dtt_ref, slab_ref, slab4_ref, *, tm):
    xb = x_ref[...].astype(BF16)
    dot = functools.partial(jnp.dot, preferred_element_type=F32)

    qkv = dot(xb, wqkv_ref[...])
    n_slab = 3 * D_ATTN // LANES
    per = D_ATTN // LANES
    for s in range(n_slab):
        col = qkv[:, s * LANES:(s + 1) * LANES]
        if s < per:
            col = col * (HEAD_DIM ** -0.5 * LOG2E)
        slab_ref[s] = col

    z_ref[...] = dot(xb, wz_ref[...])
    xbc_ref[...] = dot(xb, wxbc_ref[...])
    dt_ref[...] = _softplus(dot(xb, wdt_ref[...]) + dtb_ref[...])
    dtt = lax.dot_general(wdtt_ref[...], xb, (((1,), (1,)), ((), ())), preferred_element_type=F32)
    dtt = _softplus(dtt + dtbt_ref[:, 0:1])
    for c in range(tm // CHUNK):
        dtt_ref[c] = dtt[:, c * CHUNK:(c + 1) * CHUNK]

    nat = (q1_ref, k1_ref, v1_ref)
    d4 = (q4_ref, k4_ref, v4_ref)
    d16 = (q16_ref, k16_ref, v16_ref)
    n4, n16 = tm // 4, tm // 16
    for s in range(n_slab):
        a, c = divmod(s, per)
        lanes = slice(c * LANES, (c + 1) * LANES)
        nat[a][:, lanes] = slab_ref[s].astype(BF16)
        for r in range(4):
            part = slab_ref[s, pl.ds(r, n4, stride=4), :]
            d4[a][r, :, lanes] = part.astype(BF16)
            slab4_ref[s, r * n4:(r + 1) * n4, :] = part
        for r in range(4):
            for hi in range(4):
                part = slab4_ref[s, pl.ds(r * n4 + hi, n16, stride=4), :]
                d16[a][r + 4 * hi, :, lanes] = part.astype(BF16)


def _in_proj(x, wqkv, wz, wxbc, wdt, wdtt, dtb, dtbt):
    nseq, seq, _ = x.shape
    tm = min(TM_PROJ, seq)
    grid = (nseq, seq // tm)
    row = lambda w: pl.BlockSpec((None, tm, w), lambda n, j: (n, j, 0))
    res = lambda d: pl.BlockSpec((None, d, tm // d, D_ATTN), lambda n, j: (n, 0, j, 0))
    sds = jax.ShapeDtypeStruct
    out_shape = (
        [sds((nseq, seq, D_ATTN), BF16)] * 3
        + [sds((nseq, 4, seq // 4, D_ATTN), BF16)] * 3
        + [sds((nseq, 16, seq // 16, D_ATTN), BF16)] * 3
        + [sds((nseq, seq, D_SSD), F32), sds((nseq, seq, D_XBC), F32), sds((nseq, seq, LANES), F32),
           sds((nseq, seq // CHUNK, 16, CHUNK), F32)]
    )
    out_specs = (
        [row(D_ATTN)] * 3 + [res(4)] * 3 + [res(16)] * 3
        + [row(D_SSD), row(D_XBC), row(LANES),
           pl.BlockSpec((None, tm // CHUNK, 16, CHUNK), lambda n, j: (n, j, 0, 0))]
    )
    in_specs = [row(D_MODEL)] + [_const_spec(w.shape) for w in (wqkv, wz, wxbc, wdt, wdtt, dtb, dtbt)]
    return pl.pallas_call(
        functools.partial(_in_proj_kernel, tm=tm),
        grid=grid, in_specs=in_specs, out_specs=out_specs, out_shape=out_shape,
        scratch_shapes=[pltpu.VMEM((3 * D_ATTN // LANES, tm, LANES), F32)] * 2,
        compiler_params=_params(2), name="in_proj",
    )(x, wqkv, wz, wxbc, wdt, wdtt, dtb, dtbt)


def _conv_silu_chunk(ext_ref, convw_ref, convb_ref, c):
    span = CHUNK + 2 * HALO
    halves = []
    for half in range(D_XBC // 512):
        lanes = slice(half * 512, (half + 1) * 512)
        xe = ext_ref[c * CHUNK:c * CHUNK + span, lanes]
        acc = jnp.broadcast_to(convb_ref[:, lanes], (span, 512))
        for k in range(SSD_CONV):
            shift = (SSD_CONV // 2 - k) % span
            xk = xe if shift == 0 else pltpu.roll(xe, shift, 0)
            acc = acc + convw_ref[k:k + 1, lanes] * xk
        acc = acc[HALO:HALO + CHUNK]
        halves.append(acc * _sigmoid(acc))
    return jnp.concatenate(halves, axis=1)


def _ssd_prelude(xs, bm32, cm32, dt, dtt, a_row, a_col, tcol, trow, *, backward):
    head0 = N_HEADS if backward else 0
    tot_row = 0 if backward else CHUNK - 1
    qi = lax.broadcasted_iota(jnp.int32, (CHUNK, CHUNK), 0)
    si = lax.broadcasted_iota(jnp.int32, (CHUNK, CHUNK), 1)
    causal = (si >= qi) if backward else (si <= qi)
    lt64 = _lane_lt64((CHUNK, LANES))
    nt = (((1,), (1,)), ((), ()))
    acum = _exact_dot_r(tcol, dt * a_row)
    acum_t = _exact_dot_l(dtt * a_col, trow)
    bm = [b.astype(BF16) for b in bm32]
    cm = [c.astype(BF16) for c in cm32]
    cb = [lax.dot_general(cm[g], bm[g], nt, preferred_element_type=F32) for g in range(SSD_GROUPS)]
    pairs = []
    for p in range(N_HEADS // 2):
        g = p // 2
        cols = []
        for h in (2 * p, 2 * p + 1):
            lane = head0 + h
            cols.append((jnp.broadcast_to(acum[:, lane:lane + 1], (CHUNK, LANES)),
                         jnp.broadcast_to(dt[:, lane:lane + 1], (CHUNK, LANES))))
        a64 = jnp.where(lt64, cols[0][0], cols[1][0])
        dt64 = jnp.where(lt64, cols[0][1], cols[1][1])
        tot = a64[tot_row:tot_row + 1, :]
        xdt = xs[:, p * LANES:(p + 1) * LANES] * dt64
        lmats = []
        for i, h in enumerate((2 * p, 2 * p + 1)):
            arow = jnp.broadcast_to(acum_t[head0 + h:head0 + h + 1, :], (CHUNK, CHUNK))
            decay = jnp.exp(jnp.where(causal, cols[i][0] - arow, -jnp.inf))
            lmats.append((cb[g] * decay).astype(BF16))
        pairs.append((lmats, xdt.astype(BF16), (xdt * jnp.exp(tot - a64)).astype(BF16), jnp.exp(tot),
                      jnp.exp(a64)))
    return xs, bm, cm, pairs


def _ssd_body(pre, state_ref):
    xs, bm, cm, pairs = pre
    lt64 = _lane_lt64((CHUNK, LANES))
    dot = functools.partial(jnp.dot, preferred_element_type=F32)
    y_off = [dot(cm[g], state_ref[g].astype(BF16)) for g in range(SSD_GROUPS)]
    y_tiles = []
    for p, (lmats, xdt_b, _, _, ea64) in enumerate(pairs):
        ys = [dot(lm, xdt_b) for lm in lmats]
        off = y_off[p // 2][:, (p % 2) * LANES:(p % 2 + 1) * LANES]
        y_tiles.append(jnp.where(lt64, ys[0], ys[1]) + off * ea64)
    for g in range(SSD_GROUPS):
        xw = jnp.concatenate([pairs[2 * g][2], pairs[2 * g + 1][2]], axis=1)
        cd = jnp.concatenate([pairs[2 * g][3], pairs[2 * g + 1][3]], axis=1)
        upd = lax.dot_general(bm[g], xw, (((0,), (0,)), ((), ())), preferred_element_type=F32)
        state_ref[g] = cd * state_ref[g] + upd
    return jnp.concatenate(y_tiles, axis=1)


def _ssd_fwd_kernel(xprev_ref, xcur_ref, xnext_ref, dt_ref, dtt_ref, convw_ref, convb_ref,
                    alog_ref, alogt_ref, tcol_ref, trow_ref, dskip_ref,
                    act_ref, yf_ref, ext_ref, state_ref, *, rows, nblk):
    j = pl.program_id(1)

    @pl.when(j == 0)
    def _():
        state_ref[...] = jnp.zeros_like(state_ref)

    keep_prev = jnp.where(j == 0, 0.0, 1.0)
    keep_next = jnp.where(j == nblk - 1, 0.0, 1.0)
    ext_ref[0:HALO, :] = xprev_ref[...] * keep_prev
    ext_ref[HALO:HALO + rows, :] = xcur_ref[...]
    ext_ref[HALO + rows:2 * HALO + rows, :] = xnext_ref[...] * keep_next

    a_row = -jnp.exp(alog_ref[...])
    a_col = -jnp.exp(alogt_ref[:, 0:1])

    def prelude(c):
        rs = slice(c * CHUNK, (c + 1) * CHUNK)
        act = _conv_silu_chunk(ext_ref, convw_ref, convb_ref, c)
        act_ref[rs, :] = act.astype(act_ref.dtype)
        xs = act[:, 0:D_SSD]
        bm = [act[:, D_SSD + g * SSD_STATE:D_SSD + (g + 1) * SSD_STATE] for g in range(SSD_GROUPS)]
        cm = [act[:, D_SSD + (SSD_GROUPS + g) * SSD_STATE:D_SSD + (SSD_GROUPS + g + 1) * SSD_STATE]
              for g in range(SSD_GROUPS)]
        return _ssd_prelude(xs, bm, cm, dt_ref[rs, :], dtt_ref[c], a_row, a_col, tcol_ref[...], trow_ref[...],
                            backward=False)

    order = list(range(rows // CHUNK))
    pre = prelude(order[0])
    for i, c in enumerate(order):
        nxt = prelude(order[i + 1]) if i + 1 < len(order) else None
        y = _ssd_body(pre, state_ref)
        yf_ref[c * CHUNK:(c + 1) * CHUNK, :] = y + dskip_ref[...] * pre[0]
        pre = nxt


def _ssd_bwd_kernel(act_ref, dt_ref, dtt_ref, alog_ref, alogt_ref, tcol_ref, trow_ref,
                    yf_ref, z_ref, gnorm_ref, out_ref, state_ref, *, rows):
    j = pl.program_id(1)

    @pl.when(j == 0)
    def _():
        state_ref[...] = jnp.zeros_like(state_ref)

    a_row = -jnp.exp(alog_ref[...])
    a_col = -jnp.exp(alogt_ref[:, 0:1])

    def prelude(c):
        rs = slice(c * CHUNK, (c + 1) * CHUNK)
        xs = act_ref[rs, 0:D_SSD].astype(F32)
        bm = [act_ref[rs, D_SSD + g * SSD_STATE:D_SSD + (g + 1) * SSD_STATE] for g in range(SSD_GROUPS)]
        cm = [act_ref[rs, D_SSD + (SSD_GROUPS + g) * SSD_STATE:D_SSD + (SSD_GROUPS + g + 1) * SSD_STATE]
              for g in range(SSD_GROUPS)]
        return _ssd_prelude(xs, bm, cm, dt_ref[rs, :], dtt_ref[c], a_row, a_col, tcol_ref[...], trow_ref[...],
                            backward=True)

    order = list(reversed(range(rows // CHUNK)))
    pre = prelude(order[0])
    for i, c in enumerate(order):
        rs = slice(c * CHUNK, (c + 1) * CHUNK)
        nxt = prelude(order[i + 1]) if i + 1 < len(order) else None
        y = _ssd_body(pre, state_ref)
        zz = z_ref[rs, :]
        tot = (yf_ref[rs, :] + y) * (zz * _sigmoid(zz))
        inv = lax.rsqrt(jnp.mean(tot * tot, axis=-1, keepdims=True) + EPS)
        out_ref[rs, :] = (tot * inv * gnorm_ref[...]).astype(out_ref.dtype)
        pre = nxt


def _ssd(xbc, dt, dtt, z, consts_fwd, consts_bwd, dskip, gnorm):
    nseq, seq, _ = xbc.shape
    rows = min(R_SSD, seq)
    nblk = seq // rows
    hb = rows // HALO
    nh = seq // HALO
    state = pltpu.VMEM((SSD_GROUPS, SSD_STATE, 2 * LANES), F32)

    def specs(blk):
        cur = lambda w: pl.BlockSpec((None, rows, w), lambda n, j: (n, blk(j), 0))
        dtt_spec = pl.BlockSpec((None, rows // CHUNK, 16, CHUNK), lambda n, j: (n, blk(j), 0, 0))
        return cur, dtt_spec

    cur, dtt_spec = specs(lambda j: j)
    act, y_fwd = pl.pallas_call(
        functools.partial(_ssd_fwd_kernel, rows=rows, nblk=nblk),
        grid=(nseq, nblk),
        in_specs=[pl.BlockSpec((None, HALO, D_XBC), lambda n, j: (n, jnp.maximum(j * hb - 1, 0), 0)),
                  cur(D_XBC),
                  pl.BlockSpec((None, HALO, D_XBC), lambda n, j: (n, jnp.minimum((j + 1) * hb, nh - 1), 0)),
                  cur(LANES), dtt_spec]
        + [_const_spec(c.shape) for c in consts_fwd] + [_const_spec(dskip.shape)],
        out_specs=[cur(D_XBC), cur(D_SSD)],
        out_shape=[jax.ShapeDtypeStruct((nseq, seq, D_XBC), BF16),
                   jax.ShapeDtypeStruct((nseq, seq, D_SSD), F32)],
        scratch_shapes=[pltpu.VMEM((rows + 2 * HALO, D_XBC), F32), state],
        compiler_params=_params(2), name="ssd_fwd",
    )(xbc, xbc, xbc, dt, dtt, *consts_fwd, dskip)

    cur, dtt_spec = specs(lambda j: nblk - 1 - j)
    return pl.pallas_call(
        functools.partial(_ssd_bwd_kernel, rows=rows),
        grid=(nseq, nblk),
        in_specs=[cur(D_XBC), cur(LANES), dtt_spec] + [_const_spec(c.shape) for c in consts_bwd]
        + [cur(D_SSD), cur(D_SSD), _const_spec(gnorm.shape)],
        out_specs=cur(D_SSD),
        out_shape=jax.ShapeDtypeStruct((nseq, seq, D_SSD), BF16),
        scratch_shapes=[state],
        compiler_params=_params(2), name="ssd_bwd",
    )(act, dt, dtt, *consts_bwd, y_fwd, z, gnorm)


def _attn_kernel(q_ref, kprev_ref, kcur_ref, knext_ref, vprev_ref, vcur_ref, vnext_ref, bias_ref,
                 o_ref, lse_ref, kbuf_ref, vbuf_ref, *, rows, nblk):
    j = pl.program_id(1)
    first = j == 0
    last = j == nblk - 1
    for buf, prev, cur, nxt in ((kbuf_ref, kprev_ref, kcur_ref, knext_ref),
                                (vbuf_ref, vprev_ref, vcur_ref, vnext_ref)):
        buf[0:SIDE, :] = prev[...]
        buf[SIDE:SIDE + rows, :] = cur[...]
        buf[SIDE + rows:2 * SIDE + rows, :] = nxt[...]
    win = QB + 2 * SIDE
    lt64 = _lane_lt64((QB, LANES))
    lane = lax.broadcasted_iota(jnp.int32, (QB, LANES), 1)
    col = lax.broadcasted_iota(jnp.int32, (2 * QB, win), 1)
    nqb = rows // QB
    blocks = [(qb, p) for qb in range(nqb) for p in range(N_HEADS // 2)]

    def logits(qb, p):
        lanes = slice(p * LANES, (p + 1) * LANES)
        qp = q_ref[qb * QB:(qb + 1) * QB, lanes]
        zero = jnp.zeros_like(qp)
        q2 = jnp.concatenate([jnp.where(lt64, qp, zero), jnp.where(lt64, zero, qp)], axis=0)
        s = lax.dot_general(q2, kbuf_ref[qb * QB:qb * QB + win, lanes], (((1,), (1,)), ((), ())),
                            preferred_element_type=F32) + bias_ref[p]
        if qb == 0:
            s = jnp.where(jnp.logical_and(col < SIDE, first), NEG, s)
        if qb == nqb - 1:
            s = jnp.where(jnp.logical_and(col >= win - SIDE, last), NEG, s)
        return s

    pending = [logits(*blocks[i]) for i in range(min(ATT_PIPE, len(blocks)))]
    lse_pk = None
    for i, (qb, p) in enumerate(blocks):
        s = pending.pop(0)
        if i + ATT_PIPE < len(blocks):
            pending.append(logits(*blocks[i + ATT_PIPE]))
        lanes = slice(p * LANES, (p + 1) * LANES)
        qrows = slice(qb * QB, (qb + 1) * QB)
        m = jnp.max(s, axis=1, keepdims=True)
        e = jnp.exp2(s - m)
        l = jnp.sum(e, axis=1, keepdims=True)
        o2 = jnp.dot(e.astype(BF16), vbuf_ref[qb * QB:qb * QB + win, lanes], preferred_element_type=F32)
        denom = jnp.where(lt64, l[0:QB], l[QB:2 * QB])
        o_ref[qrows, lanes] = jnp.where(lt64, o2[0:QB], o2[QB:2 * QB]) / denom
        lse2 = jnp.where(lt64, m[0:QB], m[QB:2 * QB]) + jnp.log2(denom)
        part = jnp.where(jnp.logical_or(lane == _lse_lane(2 * p), lane == _lse_lane(2 * p + 1)), lse2, 0.0)
        lse_pk = part if p == 0 else lse_pk + part
        if p == N_HEADS // 2 - 1:
            lse_ref[qrows, :] = lse_pk


def _attn_branch(q, k, v, bias):
    ng, length, _ = q.shape
    rows = min(RB_ATT, length)
    nblk = length // rows
    hb = rows // SIDE
    nh = length // SIDE
    cur = lambda w: pl.BlockSpec((None, rows, w), lambda g, j: (g, j, 0))
    prev = pl.BlockSpec((None, SIDE, D_ATTN), lambda g, j: (g, jnp.maximum(j * hb - 1, 0), 0))
    nxt = pl.BlockSpec((None, SIDE, D_ATTN), lambda g, j: (g, jnp.minimum((j + 1) * hb, nh - 1), 0))
    return pl.pallas_call(
        functools.partial(_attn_kernel, rows=rows, nblk=nblk),
        grid=(ng, nblk),
        in_specs=[cur(D_ATTN), prev, cur(D_ATTN), nxt, prev, cur(D_ATTN), nxt, _const_spec(bias.shape)],
        out_specs=[cur(D_ATTN), cur(LANES)],
        out_shape=[jax.ShapeDtypeStruct((ng, length, D_ATTN), F32),
                   jax.ShapeDtypeStruct((ng, length, LANES), F32)],
        scratch_shapes=[pltpu.VMEM((rows + 2 * SIDE, D_ATTN), BF16),
                        pltpu.VMEM((rows + 2 * SIDE, D_ATTN), BF16)],
        compiler_params=_params(2), name="dilated_attn",
    )(q, k, k, k, v, v, v, bias)


def _merge_kernel(o1_ref, o4_ref, o16_ref, l1_ref, l4_ref, l16_ref, gnorm_ref, out_ref,
                  so4_ref, so16_ref, sl4_ref, sl16_ref, *, tc):
    per = D_ATTN // LANES
    for d, o_ref, l_ref, so_ref, sl_ref in ((4, o4_ref, l4_ref, so4_ref, sl4_ref),
                                            (16, o16_ref, l16_ref, so16_ref, sl16_ref)):
        for r in range(d):
            rows = pl.ds(r, tc // d, stride=d)
            sl_ref[rows, :] = l_ref[r]
            for s in range(per):
                so_ref[s, rows, :] = o_ref[r, :, s * LANES:(s + 1) * LANES]
    lt64 = _lane_lt64((QB, LANES))
    for b in range(tc // QB):
        rs = slice(b * QB, (b + 1) * QB)
        l1, l4, l16 = l1_ref[rs, :], sl4_ref[rs, :], sl16_ref[rs, :]
        m = jnp.maximum(jnp.maximum(l1, l4), l16)
        e1, e4, e16 = jnp.exp2(l1 - m), jnp.exp2(l4 - m), jnp.exp2(l16 - m)
        inv = 1.0 / (e1 + e4 + e16)
        w4, w16 = e4 * inv, e16 * inv
        tiles = []
        for p in range(per):
            lanes = slice(p * LANES, (p + 1) * LANES)
            c0, c1 = _lse_lane(2 * p), _lse_lane(2 * p + 1)
            w4p = jnp.where(lt64, jnp.broadcast_to(w4[:, c0:c0 + 1], (QB, LANES)),
                            jnp.broadcast_to(w4[:, c1:c1 + 1], (QB, LANES)))
            w16p = jnp.where(lt64, jnp.broadcast_to(w16[:, c0:c0 + 1], (QB, LANES)),
                             jnp.broadcast_to(w16[:, c1:c1 + 1], (QB, LANES)))
            o1 = o1_ref[rs, lanes]
            tiles.append(o1 + w4p * (so4_ref[p, rs, :] - o1) + w16p * (so16_ref[p, rs, :] - o1))
        o = jnp.concatenate(tiles, axis=1)
        inv_rms = lax.rsqrt(jnp.mean(o * o, axis=-1, keepdims=True) + EPS)
        out_ref[rs, :] = (o * inv_rms * gnorm_ref[...]).astype(out_ref.dtype)


def _merge(o1, o4, o16, l1, l4, l16, gnorm):
    nseq, seq, _ = o1.shape
    tc = min(TC_MERGE, seq)
    row = lambda w: pl.BlockSpec((None, tc, w), lambda n, j: (n, j, 0))
    res = lambda d, w: pl.BlockSpec((None, d, tc // d, w), lambda n, j: (n, 0, j, 0))
    return pl.pallas_call(
        functools.partial(_merge_kernel, tc=tc),
        grid=(nseq, seq // tc),
        in_specs=[row(D_ATTN), res(4, D_ATTN), res(16, D_ATTN), row(LANES), res(4, LANES), res(16, LANES),
                  _const_spec(gnorm.shape)],
        out_specs=row(D_ATTN),
        out_shape=jax.ShapeDtypeStruct((nseq, seq, D_ATTN), BF16),
        scratch_shapes=[pltpu.VMEM((D_ATTN // LANES, tc, LANES), F32),
                        pltpu.VMEM((D_ATTN // LANES, tc, LANES), F32),
                        pltpu.VMEM((tc, LANES), F32), pltpu.VMEM((tc, LANES), F32)],
        compiler_params=_params(2), name="merge",
    )(o1, o4, o16, l1, l4, l16, gnorm)


def _layer_norm(x, g, b):
    mu = jnp.mean(x, axis=-1, keepdims=True)
    xc = x - mu
    var = jnp.mean(xc * xc, axis=-1, keepdims=True)
    return xc * lax.rsqrt(var + EPS) * g + b


def _ffn_kernel(x_ref, a_ref, y_ref, woa_ref, woy_ref, g1_ref, b1_ref, wg_ref, wu_ref, wd_ref,
                g2_ref, b2_ref, out_ref):
    dot = functools.partial(jnp.dot, preferred_element_type=F32)
    mix = dot(a_ref[...], woa_ref[...]) + dot(y_ref[...], woy_ref[...])
    h1 = _layer_norm(ALPHA * x_ref[...] + mix, g1_ref[...], b1_ref[...])
    h1b = h1.astype(BF16)
    ffn = jnp.zeros_like(h1)
    for c in range(D_FF // FF_CHUNK):
        cols = slice(c * FF_CHUNK, (c + 1) * FF_CHUNK)
        gate = dot(h1b, wg_ref[:, cols])
        up = dot(h1b, wu_ref[:, cols])
        hid = (gate * _sigmoid(gate) * up).astype(BF16)
        ffn = ffn + dot(hid, wd_ref[cols, :])
    out_ref[...] = _layer_norm(ALPHA * h1 + ffn, g2_ref[...], b2_ref[...])


def _out_ffn(x, attn, y, woa, woy, g1, b1, wg, wu, wd, g2, b2):
    nseq, seq, _ = x.shape
    tm = min(TM_FFN, seq)
    row = lambda w: pl.BlockSpec((None, tm, w), lambda n, j: (n, j, 0))
    consts = (woa, woy, g1, b1, wg, wu, wd, g2, b2)
    return pl.pallas_call(
        _ffn_kernel,
        grid=(nseq, seq // tm),
        in_specs=[row(D_MODEL), row(D_ATTN), row(D_SSD)] + [_const_spec(c.shape) for c in consts],
        out_specs=row(D_MODEL),
        out_shape=jax.ShapeDtypeStruct((nseq, seq, D_MODEL), F32),
        compiler_params=_params(2), name="out_ffn",
    )(x, attn, y, *consts)


def _t5_bucket(rel):
    half = REL_BUCKETS // 2
    max_exact = half // 2
    ret = (rel > 0).astype(np.int32) * half
    n = np.abs(rel)
    large = max_exact + (np.log(np.maximum(n, 1) / max_exact)
                         / math.log(REL_MAX_DIST / max_exact) * (half - max_exact)).astype(np.int32)
    large = np.minimum(large, half - 1)
    return ret + np.where(n < max_exact, n, large)


def _bias_tables(rel_bias):
    win = QB + 2 * SIDE
    period = win + QB
    tables = []
    for d in DILATIONS:
        buckets = _t5_bucket((np.arange(2 * SIDE + 1) - SIDE) * d)
        band = rel_bias.astype(F32)[buckets].T * LOG2E
        vec = jnp.pad(band, ((0, 0), (0, period - band.shape[1])), constant_values=NEG)
        flat = jnp.tile(vec, (1, QB))[:, :QB * (period - 1)]
        toe = flat.reshape(N_HEADS, QB, period - 1)[:, :, :win]
        tables.append(toe.reshape(N_HEADS // 2, 2 * QB, win))
    return tables


def _selector_constants():
    tri = np.tril(np.ones((CHUNK, CHUNK), np.float32))
    to = lambda a: jnp.asarray(a, BF16)
    return to(tri), to(tri.T)


def _pad_lanes(v, width=LANES):
    return jnp.pad(v, ((0, 0), (0, width - v.shape[1])))


def _prepare(rel_bias, w_in, conv_w, conv_b, dt_bias_fwd, dt_bias_bwd, a_log_fwd, a_log_bwd, d_skip,
             attn_norm_g, ssd_norm_g, w_out, ln1_g, ln1_b, w_gate, w_up, w_down, ln2_g, ln2_b):
    w = w_in[0]
    o = 3 * D_ATTN
    p = {}
    p["wqkv"] = w[:, :o].astype(BF16)
    p["wz"] = w[:, o:o + D_SSD].astype(BF16)
    p["wxbc"] = w[:, o + D_SSD:o + D_SSD + D_XBC].astype(BF16)
    wdt = w[:, o + D_SSD + D_XBC:].astype(BF16)
    p["wdt"] = _pad_lanes(wdt)
    p["wdtt"] = wdt.T
    dtb = jnp.concatenate([dt_bias_fwd[0], dt_bias_bwd[0]]).astype(F32)[None, :]
    p["dtb"] = _pad_lanes(dtb)
    p["dtbt"] = jnp.broadcast_to(dtb.T, (2 * N_HEADS, LANES))
    alog = jnp.concatenate([a_log_fwd[0], a_log_bwd[0]]).astype(F32)[None, :]
    tri, tri_t = _selector_constants()
    convw = jnp.pad(conv_w[0].astype(F32), ((0, HALO - SSD_CONV), (0, 0)))
    a_tabs = (_pad_lanes(alog), jnp.broadcast_to(alog.T, (2 * N_HEADS, LANES)))
    p["ssd_fwd"] = (convw, conv_b[0].astype(F32)[None, :]) + a_tabs + (tri, tri_t)
    p["ssd_bwd"] = a_tabs + (tri_t, tri)
    p["dskip"] = jnp.repeat(d_skip[0].astype(F32), HEAD_DIM)[None, :]
    p["ssd_g"] = ssd_norm_g[0].astype(F32)[None, :]
    p["attn_g"] = attn_norm_g[0].astype(F32)[None, :]
    p["bias"] = _bias_tables(rel_bias)
    wo = w_out[0].astype(BF16)
    p["ffn"] = (wo[:D_ATTN], wo[D_ATTN:], ln1_g[0].astype(F32)[None, :], ln1_b[0].astype(F32)[None, :],
                w_gate[0].astype(BF16), w_up[0].astype(BF16), w_down[0].astype(BF16),
                ln2_g[0].astype(F32)[None, :], ln2_b[0].astype(F32)[None, :])
    return p


def _encoder_layer(x, p):
    nseq, seq, _ = x.shape
    (q1, k1, v1, q4, k4, v4, q16, k16, v16, z, xbc, dt, dtt) = _in_proj(
        x, p["wqkv"], p["wz"], p["wxbc"], p["wdt"], p["wdtt"], p["dtb"], p["dtbt"])

    y_ssd = _ssd(xbc, dt, dtt, z, p["ssd_fwd"], p["ssd_bwd"], p["dskip"], p["ssd_g"])

    outs, lses = [], []
    for d, (q, k, v), bias in zip(DILATIONS, ((q1, k1, v1), (q4, k4, v4), (q16, k16, v16)), p["bias"]):
        flat = lambda t: t.reshape(nseq * d, seq // d, D_ATTN)
        o, lse = _attn_branch(flat(q), flat(k), flat(v), bias)
        if d > 1:
            o = o.reshape(nseq, d, seq // d, D_ATTN)
            lse = lse.reshape(nseq, d, seq // d, LANES)
        outs.append(o)
        lses.append(lse)
    attn = _merge(*outs, *lses, p["attn_g"])
    return _out_ffn(x, attn, y_ssd, *p["ffn"])


def kernel(x_prompt, x_sample, rel_bias, w_in, conv_w, conv_b, dt_bias_fwd, dt_bias_bwd, a_log_fwd,
           a_log_bwd, d_skip, attn_norm_g, ssd_norm_g, w_out, ln1_g, ln1_b, w_gate, w_up, w_down,
           ln2_g, ln2_b):
    p = _prepare(rel_bias, w_in, conv_w, conv_b, dt_bias_fwd, dt_bias_bwd, a_log_fwd, a_log_bwd, d_skip,
                 attn_norm_g, ssd_norm_g, w_out, ln1_g, ln1_b, w_gate, w_up, w_down, ln2_g, ln2_b)
    return (_encoder_layer(x_prompt, p), _encoder_layer(x_sample, p))
```

```python
import functools
import math

import jax
import jax.numpy as jnp
import numpy as np
from jax import lax
from jax.experimental import pallas as pl
from jax.experimental.pallas import tpu as pltpu

F32 = jnp.float32
BF16 = jnp.bfloat16

D_MODEL = 1024
D_ATTN = 512
HEAD_DIM = 64
N_HEADS = 8
D_SSD = 512
SSD_GROUPS = 2
SSD_STATE = 128
SSD_CONV = 5
CHUNK = 128
D_FF = 2816
D_XBC = D_SSD + 2 * SSD_GROUPS * SSD_STATE
DILATIONS = (1, 4, 16)
SIDE = 64
REL_BUCKETS = 32
REL_MAX_DIST = 1024
DEPTH = 1
ALPHA = (2 * DEPTH) ** 0.25
EPS = 1e-5
NEG = -1e30
LOG2E = 1.4426950408889634

LANES = 128
HALO = 8
HALO_BLK = 16
VMEM_LIMIT = 56 * 1024 * 1024

TM_PROJ = 512
R_SSD = 512
RB_ATT = 512
QB = 128
ATT_PIPE = 3
TM_FFN = 512
FF_CHUNK = 256


def _const_spec(shape):
    nd = len(shape)
    return pl.BlockSpec(shape, lambda *_: (0,) * nd, pipeline_mode=pl.Buffered(1))


def _params(n_axes):
    return pltpu.CompilerParams(dimension_semantics=("arbitrary",) * n_axes,
                                vmem_limit_bytes=VMEM_LIMIT)


def _sigmoid(x):
    return 1.0 / (1.0 + jnp.exp(-x))


def _softplus(x):
    return jnp.maximum(x, 0.0) + jnp.log1p(jnp.exp(-jnp.abs(x)))


def _split3(x):
    hi = x.astype(BF16)
    r1 = x - hi.astype(F32)
    mid = r1.astype(BF16)
    lo = (r1 - mid.astype(F32)).astype(BF16)
    return hi, mid, lo


def _exact_dot_r(sel, x):
    hi, mid, lo = _split3(x)
    d = functools.partial(jnp.dot, preferred_element_type=F32)
    return d(sel, hi) + d(sel, mid) + d(sel, lo)


def _exact_dot_l(x, sel):
    hi, mid, lo = _split3(x)
    d = functools.partial(jnp.dot, preferred_element_type=F32)
    return d(hi, sel) + d(mid, sel) + d(lo, sel)


def _lane_lt64(shape):
    return lax.broadcasted_iota(jnp.int32, shape, len(shape) - 1) < HEAD_DIM


def _lse_lane(h):
    return h if h % 2 == 0 else HEAD_DIM + h


def _in_proj_kernel(x_ref, wqkv_ref, wz_ref, wxbc_ref, wdt_ref, wdtt_ref, dtb_ref, dtbt_ref,
                    q1_ref, k1_ref, v1_ref, q4_ref, k4_ref, v4_ref, q16_ref, k16_ref, v16_ref,
                    z_ref, xbc_ref, dt_ref, dtt_ref, slab_ref, slab4_ref, *, tm):
    xb = x_ref[...].astype(BF16)
    dot = functools.partial(jnp.dot, preferred_element_type=F32)

    qkv = dot(xb, wqkv_ref[...])
    n_slab = 3 * D_ATTN // LANES
    per = D_ATTN // LANES
    for s in range(n_slab):
        col = qkv[:, s * LANES:(s + 1) * LANES]
        if s < per:
            col = col * (HEAD_DIM ** -0.5 * LOG2E)
        slab_ref[s] = col

    z_ref[...] = dot(xb, wz_ref[...]).astype(z_ref.dtype)
    xbc_ref[...] = dot(xb, wxbc_ref[...]).astype(xbc_ref.dtype)
    dt_ref[...] = _softplus(dot(xb, wdt_ref[...]) + dtb_ref[...])
    dtt = lax.dot_general(wdtt_ref[...], xb, (((1,), (1,)), ((), ())), preferred_element_type=F32)
    dtt = _softplus(dtt + dtbt_ref[:, 0:1])
    for c in range(tm // CHUNK):
        dtt_ref[c] = dtt[:, c * CHUNK:(c + 1) * CHUNK]

    nat = (q1_ref, k1_ref, v1_ref)
    d4 = (q4_ref, k4_ref, v4_ref)
    d16 = (q16_ref, k16_ref, v16_ref)
    n4, n16 = tm // 4, tm // 16
    for s in range(n_slab):
        a, c = divmod(s, per)
        lanes = slice(c * LANES, (c + 1) * LANES)
        nat[a][:, lanes] = slab_ref[s].astype(BF16)
        for r in range(4):
            part = slab_ref[s, pl.ds(r, n4, stride=4), :]
            d4[a][r, :, lanes] = part.astype(BF16)
            slab4_ref[s, r * n4:(r + 1) * n4, :] = part
        for r in range(4):
            for hi in range(4):
                part = slab4_ref[s, pl.ds(r * n4 + hi, n16, stride=4), :]
                d16[a][r + 4 * hi, :, lanes] = part.astype(BF16)


def _in_proj(x, wqkv, wz, wxbc, wdt, wdtt, dtb, dtbt):
    nseq, seq, _ = x.shape
    tm = min(TM_PROJ, seq)
    grid = (nseq, seq // tm)
    row = lambda w: pl.BlockSpec((None, tm, w), lambda n, j: (n, j, 0))
    res = lambda d: pl.BlockSpec((None, d, tm // d, D_ATTN), lambda n, j: (n, 0, j, 0))
    sds = jax.ShapeDtypeStruct
    out_shape = (
        [sds((nseq, seq, D_ATTN), BF16)] * 3
        + [sds((nseq, 4, seq // 4, D_ATTN), BF16)] * 3
        + [sds((nseq, 16, seq // 16, D_ATTN), BF16)] * 3
        + [sds((nseq, seq, D_SSD), BF16), sds((nseq, seq, D_XBC), BF16), sds((nseq, seq, LANES), F32),
           sds((nseq, seq // CHUNK, 16, CHUNK), F32)]
    )
    out_specs = (
        [row(D_ATTN)] * 3 + [res(4)] * 3 + [res(16)] * 3
        + [row(D_SSD), row(D_XBC), row(LANES),
           pl.BlockSpec((None, tm // CHUNK, 16, CHUNK), lambda n, j: (n, j, 0, 0))]
    )
    in_specs = [row(D_MODEL)] + [_const_spec(w.shape) for w in (wqkv, wz, wxbc, wdt, wdtt, dtb, dtbt)]
    return pl.pallas_call(
        functools.partial(_in_proj_kernel, tm=tm),
        grid=grid, in_specs=in_specs, out_specs=out_specs, out_shape=out_shape,
        scratch_shapes=[pltpu.VMEM((3 * D_ATTN // LANES, tm, LANES), F32)] * 2,
        compiler_params=_params(2), name="in_proj",
    )(x, wqkv, wz, wxbc, wdt, wdtt, dtb, dtbt)


def _conv_silu_chunk(ext_ref, convw_ref, convb_ref, c):
    span = CHUNK + 2 * HALO
    halves = []
    for half in range(D_XBC // 512):
        lanes = slice(half * 512, (half + 1) * 512)
        start = HALO_BLK - HALO + c * CHUNK
        xe = ext_ref[start:start + span, lanes]
        acc = jnp.broadcast_to(convb_ref[:, lanes], (span, 512))
        for k in range(SSD_CONV):
            shift = (SSD_CONV // 2 - k) % span
            xk = xe if shift == 0 else pltpu.roll(xe, shift, 0)
            acc = acc + convw_ref[k:k + 1, lanes] * xk
        acc = acc[HALO:HALO + CHUNK]
        halves.append(acc * _sigmoid(acc))
    return jnp.concatenate(halves, axis=1)


def _ssd_prelude(xs, bm32, cm32, dt, dtt, a_row, a_col, tcol, trow, *, backward):
    head0 = N_HEADS if backward else 0
    tot_row = 0 if backward else CHUNK - 1
    qi = lax.broadcasted_iota(jnp.int32, (CHUNK, CHUNK), 0)
    si = lax.broadcasted_iota(jnp.int32, (CHUNK, CHUNK), 1)
    causal = (si >= qi) if backward else (si <= qi)
    lt64 = _lane_lt64((CHUNK, LANES))
    nt = (((1,), (1,)), ((), ()))
    acum = _exact_dot_r(tcol, dt * a_row)
    acum_t = _exact_dot_l(dtt * a_col, trow)
    bm = [b.astype(BF16) for b in bm32]
    cm = [c.astype(BF16) for c in cm32]
    cb = [lax.dot_general(cm[g], bm[g], nt, preferred_element_type=F32) for g in range(SSD_GROUPS)]
    pairs = []
    for p in range(N_HEADS // 2):
        g = p // 2
        cols = []
        for h in (2 * p, 2 * p + 1):
            lane = head0 + h
            cols.append((jnp.broadcast_to(acum[:, lane:lane + 1], (CHUNK, LANES)),
                         jnp.broadcast_to(dt[:, lane:lane + 1], (CHUNK, LANES))))
        a64 = jnp.where(lt64, cols[0][0], cols[1][0])
        dt64 = jnp.where(lt64, cols[0][1], cols[1][1])
        tot = a64[tot_row:tot_row + 1, :]
        xdt = xs[:, p * LANES:(p + 1) * LANES] * dt64
        lmats = []
        for i, h in enumerate((2 * p, 2 * p + 1)):
            arow = jnp.broadcast_to(acum_t[head0 + h:head0 + h + 1, :], (CHUNK, CHUNK))
            decay = jnp.exp(jnp.where(causal, cols[i][0] - arow, -jnp.inf))
            lmats.append((cb[g] * decay).astype(BF16))
        pairs.append((lmats, xdt.astype(BF16), (xdt * jnp.exp(tot - a64)).astype(BF16), jnp.exp(tot),
                      jnp.exp(a64)))
    return xs, bm, cm, pairs


def _ssd_body(pre, state_ref):
    xs, bm, cm, pairs = pre
    lt64 = _lane_lt64((CHUNK, LANES))
    dot = functools.partial(jnp.dot, preferred_element_type=F32)
    y_off = [dot(cm[g], state_ref[g].astype(BF16)) for g in range(SSD_GROUPS)]
    y_tiles = []
    for p, (lmats, xdt_b, _, _, ea64) in enumerate(pairs):
        ys = [dot(lm, xdt_b) for lm in lmats]
        off = y_off[p // 2][:, (p % 2) * LANES:(p % 2 + 1) * LANES]
        y_tiles.append(jnp.where(lt64, ys[0], ys[1]) + off * ea64)
    for g in range(SSD_GROUPS):
        xw = jnp.concatenate([pairs[2 * g][2], pairs[2 * g + 1][2]], axis=1)
        cd = jnp.concatenate([pairs[2 * g][3], pairs[2 * g + 1][3]], axis=1)
        upd = lax.dot_general(bm[g], xw, (((0,), (0,)), ((), ())), preferred_element_type=F32)
        state_ref[g] = cd * state_ref[g] + upd
    return jnp.concatenate(y_tiles, axis=1)


def _ssd_fwd_kernel(xprev_ref, xcur_ref, xnext_ref, dt_ref, dtt_ref, convw_ref, convb_ref,
                    alog_ref, alogt_ref, tcol_ref, trow_ref, dskip_ref,
                    act_ref, yf_ref, ext_ref, state_ref, *, rows, nblk):
    j = pl.program_id(1)

    @pl.when(j == 0)
    def _():
        state_ref[...] = jnp.zeros_like(state_ref)

    keep_prev = jnp.where(j == 0, 0.0, 1.0)
    keep_next = jnp.where(j == nblk - 1, 0.0, 1.0)
    ext_ref[0:HALO_BLK, :] = xprev_ref[...].astype(F32) * keep_prev
    ext_ref[HALO_BLK:HALO_BLK + rows, :] = xcur_ref[...].astype(F32)
    ext_ref[HALO_BLK + rows:2 * HALO_BLK + rows, :] = xnext_ref[...].astype(F32) * keep_next

    a_row = -jnp.exp(alog_ref[...])
    a_col = -jnp.exp(alogt_ref[:, 0:1])

    def prelude(c):
        rs = slice(c * CHUNK, (c + 1) * CHUNK)
        act = _conv_silu_chunk(ext_ref, convw_ref, convb_ref, c)
        act_ref[rs, :] = act.astype(act_ref.dtype)
        xs = act[:, 0:D_SSD]
        bm = [act[:, D_SSD + g * SSD_STATE:D_SSD + (g + 1) * SSD_STATE] for g in range(SSD_GROUPS)]
        cm = [act[:, D_SSD + (SSD_GROUPS + g) * SSD_STATE:D_SSD + (SSD_GROUPS + g + 1) * SSD_STATE]
              for g in range(SSD_GROUPS)]
        return _ssd_prelude(xs, bm, cm, dt_ref[rs, :], dtt_ref[c], a_row, a_col, tcol_ref[...], trow_ref[...],
                            backward=False)

    order = list(range(rows // CHUNK))
    pre = prelude(order[0])
    for i, c in enumerate(order):
        nxt = prelude(order[i + 1]) if i + 1 < len(order) else None
        y = _ssd_body(pre, state_ref)
        yf_ref[c * CHUNK:(c + 1) * CHUNK, :] = y + dskip_ref[...] * pre[0]
        pre = nxt


def _ssd_bwd_kernel(act_ref, dt_ref, dtt_ref, alog_ref, alogt_ref, tcol_ref, trow_ref,
                    yf_ref, z_ref, gnorm_ref, out_ref, state_ref, *, rows):
    j = pl.program_id(1)

    @pl.when(j == 0)
    def _():
        state_ref[...] = jnp.zeros_like(state_ref)

    a_row = -jnp.exp(alog_ref[...])
    a_col = -jnp.exp(alogt_ref[:, 0:1])

    def prelude(c):
        rs = slice(c * CHUNK, (c + 1) * CHUNK)
        xs = act_ref[rs, 0:D_SSD].astype(F32)
        bm = [act_ref[rs, D_SSD + g * SSD_STATE:D_SSD + (g + 1) * SSD_STATE] for g in range(SSD_GROUPS)]
        cm = [act_ref[rs, D_SSD + (SSD_GROUPS + g) * SSD_STATE:D_SSD + (SSD_GROUPS + g + 1) * SSD_STATE]
              for g in range(SSD_GROUPS)]
        return _ssd_prelude(xs, bm, cm, dt_ref[rs, :], dtt_ref[c], a_row, a_col, tcol_ref[...], trow_ref[...],
                            backward=True)

    order = list(reversed(range(rows // CHUNK)))
    pre = prelude(order[0])
    for i, c in enumerate(order):
        rs = slice(c * CHUNK, (c + 1) * CHUNK)
        nxt = prelude(order[i + 1]) if i + 1 < len(order) else None
        y = _ssd_body(pre, state_ref)
        zz = z_ref[rs, :].astype(F32)
        tot = (yf_ref[rs, :] + y) * (zz * _sigmoid(zz))
        inv = lax.rsqrt(jnp.mean(tot * tot, axis=-1, keepdims=True) + EPS)
        out_ref[rs, :] = (tot * inv * gnorm_ref[...]).astype(out_ref.dtype)
        pre = nxt


def _ssd(xbc, dt, dtt, z, consts_fwd, consts_bwd, dskip, gnorm):
    nseq, seq, _ = xbc.shape
    rows = min(R_SSD, seq)
    nblk = seq // rows
    hb = rows // HALO_BLK
    nh = seq // HALO_BLK
    state = pltpu.VMEM((SSD_GROUPS, SSD_STATE, 2 * LANES), F32)

    def specs(blk):
        cur = lambda w: pl.BlockSpec((None, rows, w), lambda n, j: (n, blk(j), 0))
        dtt_spec = pl.BlockSpec((None, rows // CHUNK, 16, CHUNK), lambda n, j: (n, blk(j), 0, 0))
        return cur, dtt_spec

    cur, dtt_spec = specs(lambda j: j)
    act, y_fwd = pl.pallas_call(
        functools.partial(_ssd_fwd_kernel, rows=rows, nblk=nblk),
        grid=(nseq, nblk),
        in_specs=[pl.BlockSpec((None, HALO_BLK, D_XBC), lambda n, j: (n, jnp.maximum(j * hb - 1, 0), 0)),
                  cur(D_XBC),
                  pl.BlockSpec((None, HALO_BLK, D_XBC), lambda n, j: (n, jnp.minimum((j + 1) * hb, nh - 1), 0)),
                  cur(LANES), dtt_spec]
        + [_const_spec(c.shape) for c in consts_fwd] + [_const_spec(dskip.shape)],
        out_specs=[cur(D_XBC), cur(D_SSD)],
        out_shape=[jax.ShapeDtypeStruct((nseq, seq, D_XBC), BF16),
                   jax.ShapeDtypeStruct((nseq, seq, D_SSD), F32)],
        scratch_shapes=[pltpu.VMEM((rows + 2 * HALO_BLK, D_XBC), F32), state],
        compiler_params=_params(2), name="ssd_fwd",
    )(xbc, xbc, xbc, dt, dtt, *consts_fwd, dskip)

    cur, dtt_spec = specs(lambda j: nblk - 1 - j)
    return pl.pallas_call(
        functools.partial(_ssd_bwd_kernel, rows=rows),
        grid=(nseq, nblk),
        in_specs=[cur(D_XBC), cur(LANES), dtt_spec] + [_const_spec(c.shape) for c in consts_bwd]
        + [cur(D_SSD), cur(D_SSD), _const_spec(gnorm.shape)],
        out_specs=cur(D_SSD),
        out_shape=jax.ShapeDtypeStruct((nseq, seq, D_SSD), BF16),
        scratch_shapes=[state],
        compiler_params=_params(2), name="ssd_bwd",
    )(act, dt, dtt, *consts_bwd, y_fwd, z, gnorm)


def _attn_kernel(q_ref, kprev_ref, kcur_ref, knext_ref, vprev_ref, vcur_ref, vnext_ref, bias_ref,
                 o_ref, lse_ref, kbuf_ref, vbuf_ref, *, rows, nblk):
    j = pl.program_id(1)
    first = j == 0
    last = j == nblk - 1
    for buf, prev, cur, nxt in ((kbuf_ref, kprev_ref, kcur_ref, knext_ref),
                                (vbuf_ref, vprev_ref, vcur_ref, vnext_ref)):
        buf[0:SIDE, :] = prev[...]
        buf[SIDE:SIDE + rows, :] = cur[...]
        buf[SIDE + rows:2 * SIDE + rows, :] = nxt[...]
    win = QB + 2 * SIDE
    lt64 = _lane_lt64((QB, LANES))
    lane = lax.broadcasted_iota(jnp.int32, (QB, LANES), 1)
    col = lax.broadcasted_iota(jnp.int32, (2 * QB, win), 1)
    nqb = rows // QB
    blocks = [(qb, p) for qb in range(nqb) for p in range(N_HEADS // 2)]

    def logits(qb, p):
        lanes = slice(p * LANES, (p + 1) * LANES)
        qp = q_ref[qb * QB:(qb + 1) * QB, lanes]
        zero = jnp.zeros_like(qp)
        q2 = jnp.concatenate([jnp.where(lt64, qp, zero), jnp.where(lt64, zero, qp)], axis=0)
        s = lax.dot_general(q2, kbuf_ref[qb * QB:qb * QB + win, lanes], (((1,), (1,)), ((), ())),
                            preferred_element_type=F32) + bias_ref[p]
        if qb == 0:
            s = jnp.where(jnp.logical_and(col < SIDE, first), NEG, s)
        if qb == nqb - 1:
            s = jnp.where(jnp.logical_and(col >= win - SIDE, last), NEG, s)
        return s

    pending = [logits(*blocks[i]) for i in range(min(ATT_PIPE, len(blocks)))]
    lse_pk = None
    for i, (qb, p) in enumerate(blocks):
        s = pending.pop(0)
        if i + ATT_PIPE < len(blocks):
            pending.append(logits(*blocks[i + ATT_PIPE]))
        lanes = slice(p * LANES, (p + 1) * LANES)
        qrows = slice(qb * QB, (qb + 1) * QB)
        m = jnp.max(s, axis=1, keepdims=True)
        e = jnp.exp2(s - m)
        l = jnp.sum(e, axis=1, keepdims=True)
        o2 = jnp.dot(e.astype(BF16), vbuf_ref[qb * QB:qb * QB + win, lanes], preferred_element_type=F32)
        denom = jnp.where(lt64, l[0:QB], l[QB:2 * QB])
        o_ref[qrows, lanes] = jnp.where(lt64, o2[0:QB], o2[QB:2 * QB]) / denom
        lse2 = jnp.where(lt64, m[0:QB], m[QB:2 * QB]) + jnp.log2(denom)
        part = jnp.where(jnp.logical_or(lane == _lse_lane(2 * p), lane == _lse_lane(2 * p + 1)), lse2, 0.0)
        lse_pk = part if p == 0 else lse_pk + part
        if p == N_HEADS // 2 - 1:
            lse_ref[qrows, :] = lse_pk


def _attn_branch(q, k, v, bias):
    ng, length, _ = q.shape
    rows = min(RB_ATT, length)
    nblk = length // rows
    hb = rows // SIDE
    nh = length // SIDE
    cur = lambda w: pl.BlockSpec((None, rows, w), lambda g, j: (g, j, 0))
    prev = pl.BlockSpec((None, SIDE, D_ATTN), lambda g, j: (g, jnp.maximum(j * hb - 1, 0), 0))
    nxt = pl.BlockSpec((None, SIDE, D_ATTN), lambda g, j: (g, jnp.minimum((j + 1) * hb, nh - 1), 0))
    return pl.pallas_call(
        functools.partial(_attn_kernel, rows=rows, nblk=nblk),
        grid=(ng, nblk),
        in_specs=[cur(D_ATTN), prev, cur(D_ATTN), nxt, prev, cur(D_ATTN), nxt, _const_spec(bias.shape)],
        out_specs=[cur(D_ATTN), cur(LANES)],
        out_shape=[jax.ShapeDtypeStruct((ng, length, D_ATTN), F32),
                   jax.ShapeDtypeStruct((ng, length, LANES), F32)],
        scratch_shapes=[pltpu.VMEM((rows + 2 * SIDE, D_ATTN), BF16),
                        pltpu.VMEM((rows + 2 * SIDE, D_ATTN), BF16)],
        compiler_params=_params(2), name="dilated_attn",
    )(q, k, k, k, v, v, v, bias)


def _merge_tile(o1_ref, o4_ref, o16_ref, l1_ref, l4_ref, l16_ref, gnorm_ref, dst_ref,
                so4_ref, so16_ref, sl4_ref, sl16_ref, *, tm):
    per = D_ATTN // LANES
    for d, o_ref, l_ref, so_ref, sl_ref in ((4, o4_ref, l4_ref, so4_ref, sl4_ref),
                                            (16, o16_ref, l16_ref, so16_ref, sl16_ref)):
        for r in range(d):
            rows = pl.ds(r, tm // d, stride=d)
            sl_ref[rows, :] = l_ref[r]
            for s in range(per):
                so_ref[s, rows, :] = o_ref[r, :, s * LANES:(s + 1) * LANES]
    lt64 = _lane_lt64((QB, LANES))
    for b in range(tm // QB):
        rs = slice(b * QB, (b + 1) * QB)
        l1, l4, l16 = l1_ref[rs, :], sl4_ref[rs, :], sl16_ref[rs, :]
        m = jnp.maximum(jnp.maximum(l1, l4), l16)
        e1, e4, e16 = jnp.exp2(l1 - m), jnp.exp2(l4 - m), jnp.exp2(l16 - m)
        inv = 1.0 / (e1 + e4 + e16)
        w4, w16 = e4 * inv, e16 * inv
        tiles = []
        for p in range(per):
            lanes = slice(p * LANES, (p + 1) * LANES)
            c0, c1 = _lse_lane(2 * p), _lse_lane(2 * p + 1)
            w4p = jnp.where(lt64, jnp.broadcast_to(w4[:, c0:c0 + 1], (QB, LANES)),
                            jnp.broadcast_to(w4[:, c1:c1 + 1], (QB, LANES)))
            w16p = jnp.where(lt64, jnp.broadcast_to(w16[:, c0:c0 + 1], (QB, LANES)),
                             jnp.broadcast_to(w16[:, c1:c1 + 1], (QB, LANES)))
            o1 = o1_ref[rs, lanes]
            tiles.append(o1 + w4p * (so4_ref[p, rs, :] - o1) + w16p * (so16_ref[p, rs, :] - o1))
        o = jnp.concatenate(tiles, axis=1)
        inv_rms = lax.rsqrt(jnp.mean(o * o, axis=-1, keepdims=True) + EPS)
        dst_ref[rs, :] = (o * inv_rms * gnorm_ref[...]).astype(dst_ref.dtype)


def _layer_norm(x, g, b):
    mu = jnp.mean(x, axis=-1, keepdims=True)
    xc = x - mu
    var = jnp.mean(xc * xc, axis=-1, keepdims=True)
    return xc * lax.rsqrt(var + EPS) * g + b


def _tail_kernel(x_ref, y_ref, o1_ref, o4_ref, o16_ref, l1_ref, l4_ref, l16_ref, gattn_ref,
                 woa_ref, woy_ref, g1_ref, b1_ref, wg_ref, wu_ref, wd_ref, g2_ref, b2_ref,
                 out_ref, attn_ref, so4_ref, so16_ref, sl4_ref, sl16_ref, *, tm):
    t = pl.program_id(0)

    @pl.when(t == 0)
    def _():
        attn_ref[1] = jnp.zeros(attn_ref.shape[1:], attn_ref.dtype)

    dot = functools.partial(jnp.dot, preferred_element_type=F32)
    attn = attn_ref[(t + 1) % 2]
    mix = dot(attn, woa_ref[...]) + dot(y_ref[...], woy_ref[...])
    h1 = _layer_norm(ALPHA * x_ref[...] + mix, g1_ref[...], b1_ref[...])
    h1b = h1.astype(BF16)
    ffn = jnp.zeros_like(h1)
    for c in range(D_FF // FF_CHUNK):
        cols = slice(c * FF_CHUNK, (c + 1) * FF_CHUNK)
        gate = dot(h1b, wg_ref[:, cols])
        up = dot(h1b, wu_ref[:, cols])
        hid = (gate * _sigmoid(gate) * up).astype(BF16)
        ffn = ffn + dot(hid, wd_ref[cols, :])
    out_ref[...] = _layer_norm(ALPHA * h1 + ffn, g2_ref[...], b2_ref[...])

    _merge_tile(o1_ref, o4_ref, o16_ref, l1_ref, l4_ref, l16_ref, gattn_ref, attn_ref.at[t % 2],
                so4_ref, so16_ref, sl4_ref, sl16_ref, tm=tm)


def _tail(x, y, o1, o4, o16, l1, l4, l16, gattn, woa, woy, g1, b1, wg, wu, wd, g2, b2):
    nseq, seq, _ = x.shape
    tm = min(TM_FFN, seq)
    nblk = seq // tm
    ntile = nseq * nblk
    ffn_tile = lambda t: jnp.maximum(t - 1, 0)
    mrg_tile = lambda t: jnp.minimum(t, ntile - 1)
    row = lambda w, tile: pl.BlockSpec((None, tm, w), lambda t: (tile(t) // nblk, tile(t) % nblk, 0))
    res = lambda d, w: pl.BlockSpec((None, d, tm // d, w),
                                    lambda t: (mrg_tile(t) // nblk, 0, mrg_tile(t) % nblk, 0))
    consts = (gattn, woa, woy, g1, b1, wg, wu, wd, g2, b2)
    per = D_ATTN // LANES
    return pl.pallas_call(
        functools.partial(_tail_kernel, tm=tm),
        grid=(ntile + 1,),
        in_specs=[row(D_MODEL, ffn_tile), row(D_SSD, ffn_tile),
                  row(D_ATTN, mrg_tile), res(4, D_ATTN), res(16, D_ATTN),
                  row(LANES, mrg_tile), res(4, LANES), res(16, LANES)]
        + [_const_spec(c.shape) for c in consts],
        out_specs=row(D_MODEL, ffn_tile),
        out_shape=jax.ShapeDtypeStruct((nseq, seq, D_MODEL), F32),
        scratch_shapes=[pltpu.VMEM((2, tm, D_ATTN), BF16),
                        pltpu.VMEM((per, tm, LANES), F32), pltpu.VMEM((per, tm, LANES), F32),
                        pltpu.VMEM((tm, LANES), F32), pltpu.VMEM((tm, LANES), F32)],
        compiler_params=_params(1), name="merge_ffn",
    )(x, y, o1, o4, o16, l1, l4, l16, *consts)


def _t5_bucket(rel):
    half = REL_BUCKETS // 2
    max_exact = half // 2
    ret = (rel > 0).astype(np.int32) * half
    n = np.abs(rel)
    large = max_exact + (np.log(np.maximum(n, 1) / max_exact)
                         / math.log(REL_MAX_DIST / max_exact) * (half - max_exact)).astype(np.int32)
    large = np.minimum(large, half - 1)
    return ret + np.where(n < max_exact, n, large)


def _bias_tables(rel_bias):
    win = QB + 2 * SIDE
    period = win + QB
    tables = []
    for d in DILATIONS:
        buckets = _t5_bucket((np.arange(2 * SIDE + 1) - SIDE) * d)
        band = rel_bias.astype(F32)[buckets].T * LOG2E
        vec = jnp.pad(band, ((0, 0), (0, period - band.shape[1])), constant_values=NEG)
        flat = jnp.tile(vec, (1, QB))[:, :QB * (period - 1)]
        toe = flat.reshape(N_HEADS, QB, period - 1)[:, :, :win]
        tables.append(toe.reshape(N_HEADS // 2, 2 * QB, win))
    return tables


def _selector_constants():
    tri = np.tril(np.ones((CHUNK, CHUNK), np.float32))
    to = lambda a: jnp.asarray(a, BF16)
    return to(tri), to(tri.T)


def _pad_lanes(v, width=LANES):
    return jnp.pad(v, ((0, 0), (0, width - v.shape[1])))


def _prepare(rel_bias, w_in, conv_w, conv_b, dt_bias_fwd, dt_bias_bwd, a_log_fwd, a_log_bwd, d_skip,
             attn_norm_g, ssd_norm_g, w_out, ln1_g, ln1_b, w_gate, w_up, w_down, ln2_g, ln2_b):
    w = w_in[0]
    o = 3 * D_ATTN
    p = {}
    p["wqkv"] = w[:, :o].astype(BF16)
    p["wz"] = w[:, o:o + D_SSD].astype(BF16)
    p["wxbc"] = w[:, o + D_SSD:o + D_SSD + D_XBC].astype(BF16)
    wdt = w[:, o + D_SSD + D_XBC:].astype(BF16)
    p["wdt"] = _pad_lanes(wdt)
    p["wdtt"] = wdt.T
    dtb = jnp.concatenate([dt_bias_fwd[0], dt_bias_bwd[0]]).astype(F32)[None, :]
    p["dtb"] = _pad_lanes(dtb)
    p["dtbt"] = jnp.broadcast_to(dtb.T, (2 * N_HEADS, LANES))
    alog = jnp.concatenate([a_log_fwd[0], a_log_bwd[0]]).astype(F32)[None, :]
    tri, tri_t = _selector_constants()
    convw = jnp.pad(conv_w[0].astype(F32), ((0, 8 - SSD_CONV), (0, 0)))
    a_tabs = (_pad_lanes(alog), jnp.broadcast_to(alog.T, (2 * N_HEADS, LANES)))
    p["ssd_fwd"] = (convw, conv_b[0].astype(F32)[None, :]) + a_tabs + (tri, tri_t)
    p["ssd_bwd"] = a_tabs + (tri_t, tri)
    p["dskip"] = jnp.repeat(d_skip[0].astype(F32), HEAD_DIM)[None, :]
    p["ssd_g"] = ssd_norm_g[0].astype(F32)[None, :]
    p["attn_g"] = attn_norm_g[0].astype(F32)[None, :]
    p["bias"] = _bias_tables(rel_bias)
    wo = w_out[0].astype(BF16)
    p["tail"] = (p["attn_g"], wo[:D_ATTN], wo[D_ATTN:], ln1_g[0].astype(F32)[None, :], ln1_b[0].astype(F32)[None, :],
                w_gate[0].astype(BF16), w_up[0].astype(BF16), w_down[0].astype(BF16),
                ln2_g[0].astype(F32)[None, :], ln2_b[0].astype(F32)[None, :])
    return p


def _encoder_layer(x, p):
    nseq, seq, _ = x.shape
    (q1, k1, v1, q4, k4, v4, q16, k16, v16, z, xbc, dt, dtt) = _in_proj(
        x, p["wqkv"], p["wz"], p["wxbc"], p["wdt"], p["wdtt"], p["dtb"], p["dtbt"])

    y_ssd = _ssd(xbc, dt, dtt, z, p["ssd_fwd"], p["ssd_bwd"], p["dskip"], p["ssd_g"])

    outs, lses = [], []
    for d, (q, k, v), bias in zip(DILATIONS, ((q1, k1, v1), (q4, k4, v4), (q16, k16, v16)), p["bias"]):
        flat = lambda t: t.reshape(nseq * d, seq // d, D_ATTN)
        o, lse = _attn_branch(flat(q), flat(k), flat(v), bias)
        if d > 1:
            o = o.reshape(nseq, d, seq // d, D_ATTN)
            lse = lse.reshape(nseq, d, seq // d, LANES)
        outs.append(o)
        lses.append(lse)
    return _tail(x, y_ssd, *outs, *lses, *p["tail"])


def kernel(x_prompt, x_sample, rel_bias, w_in, conv_w, conv_b, dt_bias_fwd, dt_bias_bwd, a_log_fwd,
           a_log_bwd, d_skip, attn_norm_g, ssd_norm_g, w_out, ln1_g, ln1_b, w_gate, w_up, w_down,
           ln2_g, ln2_b):
    p = _prepare(rel_bias, w_in, conv_w, conv_b, dt_bias_fwd, dt_bias_bwd, a_log_fwd, a_log_bwd, d_skip,
                 attn_norm_g, ssd_norm_g, w_out, ln1_g, ln1_b, w_gate, w_up, w_down, ln2_g, ln2_b)
    return (_encoder_layer(x_prompt, p), _encoder_layer(x_sample, p))
```

```python
import functools
import math

import jax
import jax.numpy as jnp
import numpy as np
from jax import lax
from jax.experimental import pallas as pl
from jax.experimental.pallas import tpu as pltpu

F32 = jnp.float32
BF16 = jnp.bfloat16

D_MODEL = 1024
D_ATTN = 512
HEAD_DIM = 64
N_HEADS = 8
D_SSD = 512
SSD_GROUPS = 2
SSD_STATE = 128
SSD_CONV = 5
CHUNK = 128
D_FF = 2816
D_XBC = D_SSD + 2 * SSD_GROUPS * SSD_STATE
DILATIONS = (1, 4, 16)
SIDE = 64
REL_BUCKETS = 32
REL_MAX_DIST = 1024
DEPTH = 1
ALPHA = (2 * DEPTH) ** 0.25
EPS = 1e-5
NEG = -1e30
LOG2E = 1.4426950408889634

LANES = 128
HALO = 8
HALO_BLK = 16
VMEM_LIMIT = 56 * 1024 * 1024

TM_PROJ = 1024
R_SSD = 512
RB_ATT = 2048
QB = 128
ATT_PIPE = 1
TM_FFN = 512
FF_CHUNK = 256


def _const_spec(shape):
    nd = len(shape)
    return pl.BlockSpec(shape, lambda *_: (0,) * nd, pipeline_mode=pl.Buffered(1))


def _params(n_axes):
    return pltpu.CompilerParams(dimension_semantics=("arbitrary",) * n_axes,
                                vmem_limit_bytes=VMEM_LIMIT)


def _sigmoid(x):
    return 1.0 / (1.0 + jnp.exp(-x))


def _softplus(x):
    return jnp.maximum(x, 0.0) + jnp.log1p(jnp.exp(-jnp.abs(x)))


def _split3(x):
    hi = x.astype(BF16)
    r1 = x - hi.astype(F32)
    mid = r1.astype(BF16)
    lo = (r1 - mid.astype(F32)).astype(BF16)
    return hi, mid, lo


def _exact_dot_r(sel, x):
    hi, mid, lo = _split3(x)
    d = functools.partial(jnp.dot, preferred_element_type=F32)
    return d(sel, hi) + d(sel, mid) + d(sel, lo)


def _exact_dot_l(x, sel):
    hi, mid, lo = _split3(x)
    d = functools.partial(jnp.dot, preferred_element_type=F32)
    return d(hi, sel) + d(mid, sel) + d(lo, sel)


def _lane_lt64(shape):
    return lax.broadcasted_iota(jnp.int32, shape, len(shape) - 1) < HEAD_DIM


def _lse_lane(h):
    return h if h % 2 == 0 else HEAD_DIM + h


def _in_proj_kernel(x_ref, wqkv_ref, wz_ref, wxbc_ref, wdtt_ref, dtbt_ref,
                    q1_ref, k1_ref, v1_ref, q4_ref, k4_ref, v4_ref, q16_ref, k16_ref, v16_ref,
                    z_ref, xbc_ref, dt_ref, dtt_ref, slab_ref, slab4_ref, *, tm):
    xb = x_ref[...].astype(BF16)
    dot = functools.partial(jnp.dot, preferred_element_type=F32)

    dtt = lax.dot_general(wdtt_ref[...], xb, (((1,), (1,)), ((), ())), preferred_element_type=F32)
    dtt = _softplus(dtt + dtbt_ref[:, 0:1])
    for c in range(tm // CHUNK):
        dtt_ref[c] = dtt[:, c * CHUNK:(c + 1) * CHUNK]
    dtt_pad = jnp.concatenate([dtt, jnp.zeros((LANES - dtt.shape[0], tm), F32)], axis=0)
    dt_ref[...] = dtt_pad.T

    qkv = dot(xb, wqkv_ref[...])
    n_slab = 3 * D_ATTN // LANES
    per = D_ATTN // LANES
    for s in range(n_slab):
        col = qkv[:, s * LANES:(s + 1) * LANES]
        if s < per:
            col = col * (HEAD_DIM ** -0.5 * LOG2E)
        slab_ref[s] = col

    z_ref[...] = dot(xb, wz_ref[...]).astype(z_ref.dtype)
    xbc_ref[...] = dot(xb, wxbc_ref[...]).astype(xbc_ref.dtype)
    nat = (q1_ref, k1_ref, v1_ref)
    d4 = (q4_ref, k4_ref, v4_ref)
    d16 = (q16_ref, k16_ref, v16_ref)
    n4, n16 = tm // 4, tm // 16
    for s in range(n_slab):
        a, c = divmod(s, per)
        lanes = slice(c * LANES, (c + 1) * LANES)
        nat[a][:, lanes] = slab_ref[s].astype(BF16)
        for r in range(4):
            part = slab_ref[s, pl.ds(r, n4, stride=4), :]
            d4[a][r, :, lanes] = part.astype(BF16)
            slab4_ref[s, r * n4:(r + 1) * n4, :] = part
        for r in range(4):
            for hi in range(4):
                part = slab4_ref[s, pl.ds(r * n4 + hi, n16, stride=4), :]
                d16[a][r + 4 * hi, :, lanes] = part.astype(BF16)


def _in_proj(x, wqkv, wz, wxbc, wdtt, dtbt):
    nseq, seq, _ = x.shape
    tm = min(TM_PROJ, seq)
    grid = (nseq, seq // tm)
    row = lambda w: pl.BlockSpec((None, tm, w), lambda n, j: (n, j, 0))
    res = lambda d: pl.BlockSpec((None, d, tm // d, D_ATTN), lambda n, j: (n, 0, j, 0))
    sds = jax.ShapeDtypeStruct
    out_shape = (
        [sds((nseq, seq, D_ATTN), BF16)] * 3
        + [sds((nseq, 4, seq // 4, D_ATTN), BF16)] * 3
        + [sds((nseq, 16, seq // 16, D_ATTN), BF16)] * 3
        + [sds((nseq, seq, D_SSD), BF16), sds((nseq, seq, D_XBC), BF16), sds((nseq, seq, LANES), F32),
           sds((nseq, seq // CHUNK, 16, CHUNK), F32)]
    )
    out_specs = (
        [row(D_ATTN)] * 3 + [res(4)] * 3 + [res(16)] * 3
        + [row(D_SSD), row(D_XBC), row(LANES),
           pl.BlockSpec((None, tm // CHUNK, 16, CHUNK), lambda n, j: (n, j, 0, 0))]
    )
    in_specs = [row(D_MODEL)] + [_const_spec(w.shape) for w in (wqkv, wz, wxbc, wdtt, dtbt)]
    return pl.pallas_call(
        functools.partial(_in_proj_kernel, tm=tm),
        grid=grid, in_specs=in_specs, out_specs=out_specs, out_shape=out_shape,
        scratch_shapes=[pltpu.VMEM((3 * D_ATTN // LANES, tm, LANES), F32)] * 2,
        compiler_params=_params(2), name="in_proj",
    )(x, wqkv, wz, wxbc, wdtt, dtbt)


def _conv_silu_chunk(ext_ref, convw_ref, convb_ref, c):
    span = CHUNK + 2 * HALO
    halves = []
    for half in range(D_XBC // 512):
        lanes = slice(half * 512, (half + 1) * 512)
        start = HALO_BLK - HALO + c * CHUNK
        xe = ext_ref[start:start + span, lanes]
        acc = jnp.broadcast_to(convb_ref[:, lanes], (span, 512))
        for k in range(SSD_CONV):
            shift = (SSD_CONV // 2 - k) % span
            xk = xe if shift == 0 else pltpu.roll(xe, shift, 0)
            acc = acc + convw_ref[k:k + 1, lanes] * xk
        acc = acc[HALO:HALO + CHUNK]
        halves.append(acc * _sigmoid(acc))
    return jnp.concatenate(halves, axis=1)


def _ssd_prelude(xs, bm32, cm32, dt, dtt, a_row, a_col, tcol, trow, *, backward):
    head0 = N_HEADS if backward else 0
    tot_row = 0 if backward else CHUNK - 1
    qi = lax.broadcasted_iota(jnp.int32, (CHUNK, CHUNK), 0)
    si = lax.broadcasted_iota(jnp.int32, (CHUNK, CHUNK), 1)
    causal = (si >= qi) if backward else (si <= qi)
    lt64 = _lane_lt64((CHUNK, LANES))
    nt = (((1,), (1,)), ((), ()))
    acum = _exact_dot_r(tcol, dt * a_row)
    acum_t = _exact_dot_l(dtt * a_col, trow)
    bm = [b.astype(BF16) for b in bm32]
    cm = [c.astype(BF16) for c in cm32]
    cb = [lax.dot_general(cm[g], bm[g], nt, preferred_element_type=F32) for g in range(SSD_GROUPS)]
    pairs = []
    for p in range(N_HEADS // 2):
        g = p // 2
        cols = []
        for h in (2 * p, 2 * p + 1):
            lane = head0 + h
            cols.append((jnp.broadcast_to(acum[:, lane:lane + 1], (CHUNK, LANES)),
                         jnp.broadcast_to(dt[:, lane:lane + 1], (CHUNK, LANES))))
        a64 = jnp.where(lt64, cols[0][0], cols[1][0])
        dt64 = jnp.where(lt64, cols[0][1], cols[1][1])
        tot = a64[tot_row:tot_row + 1, :]
        xdt = xs[:, p * LANES:(p + 1) * LANES] * dt64
        lmats = []
        for i, h in enumerate((2 * p, 2 * p + 1)):
            arow = jnp.broadcast_to(acum_t[head0 + h:head0 + h + 1, :], (CHUNK, CHUNK))
            decay = jnp.exp(jnp.where(causal, cols[i][0] - arow, -jnp.inf))
            lmats.append((cb[g] * decay).astype(BF16))
        pairs.append((lmats, xdt.astype(BF16), (xdt * jnp.exp(tot - a64)).astype(BF16), jnp.exp(tot),
                      jnp.exp(a64)))
    return xs, bm, cm, pairs


def _ssd_body(pre, state_ref):
    xs, bm, cm, pairs = pre
    lt64 = _lane_lt64((CHUNK, LANES))
    dot = functools.partial(jnp.dot, preferred_element_type=F32)
    y_off = [dot(cm[g], state_ref[g].astype(BF16)) for g in range(SSD_GROUPS)]
    y_tiles = []
    for p, (lmats, xdt_b, _, _, ea64) in enumerate(pairs):
        ys = [dot(lm, xdt_b) for lm in lmats]
        off = y_off[p // 2][:, (p % 2) * LANES:(p % 2 + 1) * LANES]
        y_tiles.append(jnp.where(lt64, ys[0], ys[1]) + off * ea64)
    for g in range(SSD_GROUPS):
        xw = jnp.concatenate([pairs[2 * g][2], pairs[2 * g + 1][2]], axis=1)
        cd = jnp.concatenate([pairs[2 * g][3], pairs[2 * g + 1][3]], axis=1)
        upd = lax.dot_general(bm[g], xw, (((0,), (0,)), ((), ())), preferred_element_type=F32)
        state_ref[g] = cd * state_ref[g] + upd
    return jnp.concatenate(y_tiles, axis=1)


def _ssd_fwd_kernel(xprev_ref, xcur_ref, xnext_ref, dt_ref, dtt_ref, convw_ref, convb_ref,
                    alog_ref, alogt_ref, tcol_ref, trow_ref, dskip_ref,
                    act_ref, yf_ref, ext_ref, state_ref, *, rows, nblk):
    j = pl.program_id(1)

    @pl.when(j == 0)
    def _():
        state_ref[...] = jnp.zeros_like(state_ref)

    keep_prev = jnp.where(j == 0, 0.0, 1.0)
    keep_next = jnp.where(j == nblk - 1, 0.0, 1.0)
    ext_ref[0:HALO_BLK, :] = xprev_ref[...].astype(F32) * keep_prev
    ext_ref[HALO_BLK:HALO_BLK + rows, :] = xcur_ref[...].astype(F32)
    ext_ref[HALO_BLK + rows:2 * HALO_BLK + rows, :] = xnext_ref[...].astype(F32) * keep_next

    a_row = -jnp.exp(alog_ref[...])
    a_col = -jnp.exp(alogt_ref[:, 0:1])

    def prelude(c):
        rs = slice(c * CHUNK, (c + 1) * CHUNK)
        act = _conv_silu_chunk(ext_ref, convw_ref, convb_ref, c)
        act_ref[rs, :] = act.astype(act_ref.dtype)
        xs = act[:, 0:D_SSD]
        bm = [act[:, D_SSD + g * SSD_STATE:D_SSD + (g + 1) * SSD_STATE] for g in range(SSD_GROUPS)]
        cm = [act[:, D_SSD + (SSD_GROUPS + g) * SSD_STATE:D_SSD + (SSD_GROUPS + g + 1) * SSD_STATE]
              for g in range(SSD_GROUPS)]
        return _ssd_prelude(xs, bm, cm, dt_ref[rs, :], dtt_ref[c], a_row, a_col, tcol_ref[...], trow_ref[...],
                            backward=False)

    order = list(range(rows // CHUNK))
    pre = prelude(order[0])
    for i, c in enumerate(order):
        nxt = prelude(order[i + 1]) if i + 1 < len(order) else None
        y = _ssd_body(pre, state_ref)
        yf_ref[c * CHUNK:(c + 1) * CHUNK, :] = y + dskip_ref[...] * pre[0]
        pre = nxt


def _ssd_bwd_kernel(act_ref, dt_ref, dtt_ref, alog_ref, alogt_ref, tcol_ref, trow_ref,
                    yf_ref, z_ref, gnorm_ref, out_ref, state_ref, *, rows):
    j = pl.program_id(1)

    @pl.when(j == 0)
    def _():
        state_ref[...] = jnp.zeros_like(state_ref)

    a_row = -jnp.exp(alog_ref[...])
    a_col = -jnp.exp(alogt_ref[:, 0:1])

    def prelude(c):
        rs = slice(c * CHUNK, (c + 1) * CHUNK)
        xs = act_ref[rs, 0:D_SSD].astype(F32)
        bm = [act_ref[rs, D_SSD + g * SSD_STATE:D_SSD + (g + 1) * SSD_STATE] for g in range(SSD_GROUPS)]
        cm = [act_ref[rs, D_SSD + (SSD_GROUPS + g) * SSD_STATE:D_SSD + (SSD_GROUPS + g + 1) * SSD_STATE]
              for g in range(SSD_GROUPS)]
        return _ssd_prelude(xs, bm, cm, dt_ref[rs, :], dtt_ref[c], a_row, a_col, tcol_ref[...], trow_ref[...],
                            backward=True)

    order = list(reversed(range(rows // CHUNK)))
    pre = prelude(order[0])
    for i, c in enumerate(order):
        rs = slice(c * CHUNK, (c + 1) * CHUNK)
        nxt = prelude(order[i + 1]) if i + 1 < len(order) else None
        y = _ssd_body(pre, state_ref)
        zz = z_ref[rs, :].astype(F32)
        tot = (yf_ref[rs, :] + y) * (zz * _sigmoid(zz))
        inv = lax.rsqrt(jnp.mean(tot * tot, axis=-1, keepdims=True) + EPS)
        out_ref[rs, :] = (tot * inv * gnorm_ref[...]).astype(out_ref.dtype)
        pre = nxt


def _ssd(xbc, dt, dtt, z, consts_fwd, consts_bwd, dskip, gnorm):
    nseq, seq, _ = xbc.shape
    rows = min(R_SSD, seq)
    nblk = seq // rows
    hb = rows // HALO_BLK
    nh = seq // HALO_BLK
    state = pltpu.VMEM((SSD_GROUPS, SSD_STATE, 2 * LANES), F32)

    def specs(blk):
        cur = lambda w: pl.BlockSpec((None, rows, w), lambda n, j: (n, blk(j), 0))
        dtt_spec = pl.BlockSpec((None, rows // CHUNK, 16, CHUNK), lambda n, j: (n, blk(j), 0, 0))
        return cur, dtt_spec

    cur, dtt_spec = specs(lambda j: j)
    act, y_fwd = pl.pallas_call(
        functools.partial(_ssd_fwd_kernel, rows=rows, nblk=nblk),
        grid=(nseq, nblk),
        in_specs=[pl.BlockSpec((None, HALO_BLK, D_XBC), lambda n, j: (n, jnp.maximum(j * hb - 1, 0), 0)),
                  cur(D_XBC),
                  pl.BlockSpec((None, HALO_BLK, D_XBC), lambda n, j: (n, jnp.minimum((j + 1) * hb, nh - 1), 0)),
                  cur(LANES), dtt_spec]
        + [_const_spec(c.shape) for c in consts_fwd] + [_const_spec(dskip.shape)],
        out_specs=[cur(D_XBC), cur(D_SSD)],
        out_shape=[jax.ShapeDtypeStruct((nseq, seq, D_XBC), BF16),
                   jax.ShapeDtypeStruct((nseq, seq, D_SSD), F32)],
        scratch_shapes=[pltpu.VMEM((rows + 2 * HALO_BLK, D_XBC), F32), state],
        compiler_params=_params(2), name="ssd_fwd",
    )(xbc, xbc, xbc, dt, dtt, *consts_fwd, dskip)

    cur, dtt_spec = specs(lambda j: nblk - 1 - j)
    return pl.pallas_call(
        functools.partial(_ssd_bwd_kernel, rows=rows),
        grid=(nseq, nblk),
        in_specs=[cur(D_XBC), cur(LANES), dtt_spec] + [_const_spec(c.shape) for c in consts_bwd]
        + [cur(D_SSD), cur(D_SSD), _const_spec(gnorm.shape)],
        out_specs=cur(D_SSD),
        out_shape=jax.ShapeDtypeStruct((nseq, seq, D_SSD), BF16),
        scratch_shapes=[state],
        compiler_params=_params(2), name="ssd_bwd",
    )(act, dt, dtt, *consts_bwd, y_fwd, z, gnorm)


def _attn_kernel(q_ref, kprev_ref, kcur_ref, knext_ref, vprev_ref, vcur_ref, vnext_ref, bias_ref,
                 o_ref, lse_ref, kbuf_ref, vbuf_ref, *, rows, nblk):
    j = pl.program_id(1)
    first = j == 0
    last = j == nblk - 1
    for buf, prev, cur, nxt in ((kbuf_ref, kprev_ref, kcur_ref, knext_ref),
                                (vbuf_ref, vprev_ref, vcur_ref, vnext_ref)):
        buf[0:SIDE, :] = prev[...]
        buf[SIDE:SIDE + rows, :] = cur[...]
        buf[SIDE + rows:2 * SIDE + rows, :] = nxt[...]
    win = QB + 2 * SIDE
    lt64 = _lane_lt64((QB, LANES))
    lane = lax.broadcasted_iota(jnp.int32, (QB, LANES), 1)
    col = lax.broadcasted_iota(jnp.int32, (2 * QB, win), 1)
    nqb = rows // QB
    blocks = [(qb, p) for qb in range(nqb) for p in range(N_HEADS // 2)]

    def logits(qb, p):
        lanes = slice(p * LANES, (p + 1) * LANES)
        qp = q_ref[qb * QB:(qb + 1) * QB, lanes]
        zero = jnp.zeros_like(qp)
        q2 = jnp.concatenate([jnp.where(lt64, qp, zero), jnp.where(lt64, zero, qp)], axis=0)
        s = lax.dot_general(q2, kbuf_ref[qb * QB:qb * QB + win, lanes], (((1,), (1,)), ((), ())),
                            preferred_element_type=F32) + bias_ref[p]
        if qb == 0:
            s = jnp.where(jnp.logical_and(col < SIDE, first), NEG, s)
        if qb == nqb - 1:
            s = jnp.where(jnp.logical_and(col >= win - SIDE, last), NEG, s)
        return s

    pending = [logits(*blocks[i]) for i in range(min(ATT_PIPE, len(blocks)))]
    lse_pk = None
    for i, (qb, p) in enumerate(blocks):
        s = pending.pop(0)
        if i + ATT_PIPE < len(blocks):
            pending.append(logits(*blocks[i + ATT_PIPE]))
        lanes = slice(p * LANES, (p + 1) * LANES)
        qrows = slice(qb * QB, (qb + 1) * QB)
        m = jnp.max(s, axis=1, keepdims=True)
        e = jnp.exp2(s - m)
        l = jnp.sum(e, axis=1, keepdims=True)
        o2 = jnp.dot(e.astype(BF16), vbuf_ref[qb * QB:qb * QB + win, lanes], preferred_element_type=F32)
        denom = jnp.where(lt64, l[0:QB], l[QB:2 * QB])
        o_ref[qrows, lanes] = jnp.where(lt64, o2[0:QB], o2[QB:2 * QB]) / denom
        lse2 = jnp.where(lt64, m[0:QB], m[QB:2 * QB]) + jnp.log2(denom)
        part = jnp.where(jnp.logical_or(lane == _lse_lane(2 * p), lane == _lse_lane(2 * p + 1)), lse2, 0.0)
        lse_pk = part if p == 0 else lse_pk + part
        if p == N_HEADS // 2 - 1:
            lse_ref[qrows, :] = lse_pk


def _attn_branch(q, k, v, bias):
    ng, length, _ = q.shape
    rows = min(RB_ATT, length)
    nblk = length // rows
    hb = rows // SIDE
    nh = length // SIDE
    cur = lambda w: pl.BlockSpec((None, rows, w), lambda g, j: (g, j, 0))
    prev = pl.BlockSpec((None, SIDE, D_ATTN), lambda g, j: (g, jnp.maximum(j * hb - 1, 0), 0))
    nxt = pl.BlockSpec((None, SIDE, D_ATTN), lambda g, j: (g, jnp.minimum((j + 1) * hb, nh - 1), 0))
    return pl.pallas_call(
        functools.partial(_attn_kernel, rows=rows, nblk=nblk),
        grid=(ng, nblk),
        in_specs=[cur(D_ATTN), prev, cur(D_ATTN), nxt, prev, cur(D_ATTN), nxt, _const_spec(bias.shape)],
        out_specs=[cur(D_ATTN), cur(LANES)],
        out_shape=[jax.ShapeDtypeStruct((ng, length, D_ATTN), F32),
                   jax.ShapeDtypeStruct((ng, length, LANES), F32)],
        scratch_shapes=[pltpu.VMEM((rows + 2 * SIDE, D_ATTN), BF16),
                        pltpu.VMEM((rows + 2 * SIDE, D_ATTN), BF16)],
        compiler_params=_params(2), name="dilated_attn",
    )(q, k, k, k, v, v, v, bias)


def _merge_tile(o1_ref, o4_ref, o16_ref, l1_ref, l4_ref, l16_ref, gnorm_ref, dst_ref,
                so4_ref, so16_ref, sl4_ref, sl16_ref, *, tm):
    per = D_ATTN // LANES
    for d, o_ref, l_ref, so_ref, sl_ref in ((4, o4_ref, l4_ref, so4_ref, sl4_ref),
                                            (16, o16_ref, l16_ref, so16_ref, sl16_ref)):
        for r in range(d):
            rows = pl.ds(r, tm // d, stride=d)
            sl_ref[rows, :] = l_ref[r]
            for s in range(per):
                so_ref[s, rows, :] = o_ref[r, :, s * LANES:(s + 1) * LANES]
    lt64 = _lane_lt64((QB, LANES))
    for b in range(tm // QB):
        rs = slice(b * QB, (b + 1) * QB)
        l1, l4, l16 = l1_ref[rs, :], sl4_ref[rs, :], sl16_ref[rs, :]
        m = jnp.maximum(jnp.maximum(l1, l4), l16)
        e1, e4, e16 = jnp.exp2(l1 - m), jnp.exp2(l4 - m), jnp.exp2(l16 - m)
        inv = 1.0 / (e1 + e4 + e16)
        w4, w16 = e4 * inv, e16 * inv
        tiles = []
        for p in range(per):
            lanes = slice(p * LANES, (p + 1) * LANES)
            c0, c1 = _lse_lane(2 * p), _lse_lane(2 * p + 1)
            w4p = jnp.where(lt64, jnp.broadcast_to(w4[:, c0:c0 + 1], (QB, LANES)),
                            jnp.broadcast_to(w4[:, c1:c1 + 1], (QB, LANES)))
            w16p = jnp.where(lt64, jnp.broadcast_to(w16[:, c0:c0 + 1], (QB, LANES)),
                             jnp.broadcast_to(w16[:, c1:c1 + 1], (QB, LANES)))
            o1 = o1_ref[rs, lanes]
            tiles.append(o1 + w4p * (so4_ref[p, rs, :] - o1) + w16p * (so16_ref[p, rs, :] - o1))
        o = jnp.concatenate(tiles, axis=1)
        inv_rms = lax.rsqrt(jnp.mean(o * o, axis=-1, keepdims=True) + EPS)
        dst_ref[rs, :] = (o * inv_rms * gnorm_ref[...]).astype(dst_ref.dtype)


def _layer_norm(x, g, b):
    mu = jnp.mean(x, axis=-1, keepdims=True)
    xc = x - mu
    var = jnp.mean(xc * xc, axis=-1, keepdims=True)
    return xc * lax.rsqrt(var + EPS) * g + b


def _tail_kernel(x_ref, y_ref, o1_ref, o4_ref, o16_ref, l1_ref, l4_ref, l16_ref, gattn_ref,
                 woa_ref, woy_ref, g1_ref, b1_ref, wg_ref, wu_ref, wd_ref, g2_ref, b2_ref,
                 out_ref, attn_ref, so4_ref, so16_ref, sl4_ref, sl16_ref, *, tm):
    t = pl.program_id(0)

    @pl.when(t == 0)
    def _():
        attn_ref[1] = jnp.zeros(attn_ref.shape[1:], attn_ref.dtype)

    dot = functools.partial(jnp.dot, preferred_element_type=F32)
    attn = attn_ref[(t + 1) % 2]
    mix = dot(attn, woa_ref[...]) + dot(y_ref[...], woy_ref[...])
    h1 = _layer_norm(ALPHA * x_ref[...] + mix, g1_ref[...], b1_ref[...])
    h1b = h1.astype(BF16)
    ffn = jnp.zeros_like(h1)
    for c in range(D_FF // FF_CHUNK):
        cols = slice(c * FF_CHUNK, (c + 1) * FF_CHUNK)
        gate = dot(h1b, wg_ref[:, cols])
        up = dot(h1b, wu_ref[:, cols])
        hid = (gate * _sigmoid(gate) * up).astype(BF16)
        ffn = ffn + dot(hid, wd_ref[cols, :])
    out_ref[...] = _layer_norm(ALPHA * h1 + ffn, g2_ref[...], b2_ref[...])

    _merge_tile(o1_ref, o4_ref, o16_ref, l1_ref, l4_ref, l16_ref, gattn_ref, attn_ref.at[t % 2],
                so4_ref, so16_ref, sl4_ref, sl16_ref, tm=tm)


def _tail(x, y, o1, o4, o16, l1, l4, l16, gattn, woa, woy, g1, b1, wg, wu, wd, g2, b2):
    nseq, seq, _ = x.shape
    tm = min(TM_FFN, seq)
    nblk = seq // tm
    ntile = nseq * nblk
    ffn_tile = lambda t: jnp.maximum(t - 1, 0)
    mrg_tile = lambda t: jnp.minimum(t, ntile - 1)
    row = lambda w, tile: pl.BlockSpec((None, tm, w), lambda t: (tile(t) // nblk, tile(t) % nblk, 0))
    res = lambda d, w: pl.BlockSpec((None, d, tm // d, w),
                                    lambda t: (mrg_tile(t) // nblk, 0, mrg_tile(t) % nblk, 0))
    consts = (gattn, woa, woy, g1, b1, wg, wu, wd, g2, b2)
    per = D_ATTN // LANES
    return pl.pallas_call(
        functools.partial(_tail_kernel, tm=tm),
        grid=(ntile + 1,),
        in_specs=[row(D_MODEL, ffn_tile), row(D_SSD, ffn_tile),
                  row(D_ATTN, mrg_tile), res(4, D_ATTN), res(16, D_ATTN),
                  row(LANES, mrg_tile), res(4, LANES), res(16, LANES)]
        + [_const_spec(c.shape) for c in consts],
        out_specs=row(D_MODEL, ffn_tile),
        out_shape=jax.ShapeDtypeStruct((nseq, seq, D_MODEL), F32),
        scratch_shapes=[pltpu.VMEM((2, tm, D_ATTN), BF16),
                        pltpu.VMEM((per, tm, LANES), F32), pltpu.VMEM((per, tm, LANES), F32),
                        pltpu.VMEM((tm, LANES), F32), pltpu.VMEM((tm, LANES), F32)],
        compiler_params=_params(1), name="merge_ffn",
    )(x, y, o1, o4, o16, l1, l4, l16, *consts)


def _t5_bucket(rel):
    half = REL_BUCKETS // 2
    max_exact = half // 2
    ret = (rel > 0).astype(np.int32) * half
    n = np.abs(rel)
    large = max_exact + (np.log(np.maximum(n, 1) / max_exact)
                         / math.log(REL_MAX_DIST / max_exact) * (half - max_exact)).astype(np.int32)
    large = np.minimum(large, half - 1)
    return ret + np.where(n < max_exact, n, large)


def _bias_tables(rel_bias):
    win = QB + 2 * SIDE
    period = win + QB
    tables = []
    for d in DILATIONS:
        buckets = _t5_bucket((np.arange(2 * SIDE + 1) - SIDE) * d)
        band = rel_bias.astype(F32)[buckets].T * LOG2E
        vec = jnp.pad(band, ((0, 0), (0, period - band.shape[1])), constant_values=NEG)
        flat = jnp.tile(vec, (1, QB))[:, :QB * (period - 1)]
        toe = flat.reshape(N_HEADS, QB, period - 1)[:, :, :win]
        tables.append(toe.reshape(N_HEADS // 2, 2 * QB, win))
    return tables


def _selector_constants():
    tri = np.tril(np.ones((CHUNK, CHUNK), np.float32))
    to = lambda a: jnp.asarray(a, BF16)
    return to(tri), to(tri.T)


def _pad_lanes(v, width=LANES):
    return jnp.pad(v, ((0, 0), (0, width - v.shape[1])))


def _prepare(rel_bias, w_in, conv_w, conv_b, dt_bias_fwd, dt_bias_bwd, a_log_fwd, a_log_bwd, d_skip,
             attn_norm_g, ssd_norm_g, w_out, ln1_g, ln1_b, w_gate, w_up, w_down, ln2_g, ln2_b):
    w = w_in[0]
    o = 3 * D_ATTN
    p = {}
    p["wqkv"] = w[:, :o].astype(BF16)
    p["wz"] = w[:, o:o + D_SSD].astype(BF16)
    p["wxbc"] = w[:, o + D_SSD:o + D_SSD + D_XBC].astype(BF16)
    wdt = w[:, o + D_SSD + D_XBC:].astype(BF16)
    p["wdtt"] = wdt.T
    dtb = jnp.concatenate([dt_bias_fwd[0], dt_bias_bwd[0]]).astype(F32)[None, :]
    p["dtbt"] = jnp.broadcast_to(dtb.T, (2 * N_HEADS, LANES))
    alog = jnp.concatenate([a_log_fwd[0], a_log_bwd[0]]).astype(F32)[None, :]
    tri, tri_t = _selector_constants()
    convw = jnp.pad(conv_w[0].astype(F32), ((0, 8 - SSD_CONV), (0, 0)))
    a_tabs = (_pad_lanes(alog), jnp.broadcast_to(alog.T, (2 * N_HEADS, LANES)))
    p["ssd_fwd"] = (convw, conv_b[0].astype(F32)[None, :]) + a_tabs + (tri, tri_t)
    p["ssd_bwd"] = a_tabs + (tri_t, tri)
    p["dskip"] = jnp.repeat(d_skip[0].astype(F32), HEAD_DIM)[None, :]
    p["ssd_g"] = ssd_norm_g[0].astype(F32)[None, :]
    p["attn_g"] = attn_norm_g[0].astype(F32)[None, :]
    p["bias"] = _bias_tables(rel_bias)
    wo = w_out[0].astype(BF16)
    p["tail"] = (p["attn_g"], wo[:D_ATTN], wo[D_ATTN:], ln1_g[0].astype(F32)[None, :], ln1_b[0].astype(F32)[None, :],
                w_gate[0].astype(BF16), w_up[0].astype(BF16), w_down[0].astype(BF16),
                ln2_g[0].astype(F32)[None, :], ln2_b[0].astype(F32)[None, :])
    return p


def _encoder_layer(x, p):
    nseq, seq, _ = x.shape
    (q1, k1, v1, q4, k4, v4, q16, k16, v16, z, xbc, dt, dtt) = _in_proj(
        x, p["wqkv"], p["wz"], p["wxbc"], p["wdtt"], p["dtbt"])

    y_ssd = _ssd(xbc, dt, dtt, z, p["ssd_fwd"], p["ssd_bwd"], p["dskip"], p["ssd_g"])

    outs, lses = [], []
    for d, (q, k, v), bias in zip(DILATIONS, ((q1, k1, v1), (q4, k4, v4), (q16, k16, v16)), p["bias"]):
        flat = lambda t: t.reshape(nseq * d, seq // d, D_ATTN)
        o, lse = _attn_branch(flat(q), flat(k), flat(v), bias)
        if d > 1:
            o = o.reshape(nseq, d, seq // d, D_ATTN)
            lse = lse.reshape(nseq, d, seq // d, LANES)
        outs.append(o)
        lses.append(lse)
    return _tail(x, y_ssd, *outs, *lses, *p["tail"])


def kernel(x_prompt, x_sample, rel_bias, w_in, conv_w, conv_b, dt_bias_fwd, dt_bias_bwd, a_log_fwd,
           a_log_bwd, d_skip, attn_norm_g, ssd_norm_g, w_out, ln1_g, ln1_b, w_gate, w_up, w_down,
           ln2_g, ln2_b):
    p = _prepare(rel_bias, w_in, conv_w, conv_b, dt_bias_fwd, dt_bias_bwd, a_log_fwd, a_log_bwd, d_skip,
                 attn_norm_g, ssd_norm_g, w_out, ln1_g, ln1_b, w_gate, w_up, w_down, ln2_g, ln2_b)
    return (_encoder_layer(x_prompt, p), _encoder_layer(x_sample, p))
```

```python
import functools
import math

import jax
import jax.numpy as jnp
import numpy as np
from jax import lax
from jax.experimental import pallas as pl
from jax.experimental.pallas import tpu as pltpu

F32 = jnp.float32
BF16 = jnp.bfloat16

D_MODEL = 1024
D_ATTN = 512
HEAD_DIM = 64
N_HEADS = 8
D_SSD = 512
SSD_GROUPS = 2
SSD_STATE = 128
SSD_CONV = 5
CHUNK = 128
D_FF = 2816
D_XBC = D_SSD + 2 * SSD_GROUPS * SSD_STATE
DILATIONS = (1, 4, 16)
SIDE = 64
REL_BUCKETS = 32
REL_MAX_DIST = 1024
DEPTH = 1
ALPHA = (2 * DEPTH) ** 0.25
EPS = 1e-5
NEG = -1e30
LOG2E = 1.4426950408889634

LANES = 128
HALO_BLK = 16
VMEM_LIMIT = 56 * 1024 * 1024

TM_PROJ = 1024
R_SSD = 512
RB_ATT = 2048
QB = 128
ATT_PIPE = 1
TM_FFN = 512
FF_CHUNK = 256


def _const_spec(shape):
    nd = len(shape)
    return pl.BlockSpec(shape, lambda *_: (0,) * nd, pipeline_mode=pl.Buffered(1))


def _params(n_axes):
    return pltpu.CompilerParams(dimension_semantics=("arbitrary",) * n_axes,
                                vmem_limit_bytes=VMEM_LIMIT)


def _sigmoid(x):
    return 1.0 / (1.0 + jnp.exp(-x))


def _softplus(x):
    return jnp.maximum(x, 0.0) + jnp.log1p(jnp.exp(-jnp.abs(x)))


def _split3(x):
    hi = x.astype(BF16)
    r1 = x - hi.astype(F32)
    mid = r1.astype(BF16)
    lo = (r1 - mid.astype(F32)).astype(BF16)
    return hi, mid, lo


def _exact_dot_r(sel, x):
    hi, mid, lo = _split3(x)
    d = functools.partial(jnp.dot, preferred_element_type=F32)
    return d(sel, hi) + d(sel, mid) + d(sel, lo)


def _exact_dot_l(x, sel):
    hi, mid, lo = _split3(x)
    d = functools.partial(jnp.dot, preferred_element_type=F32)
    return d(hi, sel) + d(mid, sel) + d(lo, sel)


def _lane_lt64(shape):
    return lax.broadcasted_iota(jnp.int32, shape, len(shape) - 1) < HEAD_DIM


def _lse_lane(h):
    return h if h % 2 == 0 else HEAD_DIM + h


def _in_proj_kernel(x_ref, wqkv_ref, wz_ref, wxbc_ref, wdtt_ref, dtbt_ref,
                    q1_ref, k1_ref, v1_ref, q4_ref, k4_ref, v4_ref, q16_ref, k16_ref, v16_ref,
                    z_ref, xbc_ref, dt_ref, dtt_ref, slab_ref, slab4_ref, *, tm):
    xb = x_ref[...].astype(BF16)
    dot = functools.partial(jnp.dot, preferred_element_type=F32)

    dtt = lax.dot_general(wdtt_ref[...], xb, (((1,), (1,)), ((), ())), preferred_element_type=F32)
    dtt = _softplus(dtt + dtbt_ref[:, 0:1])
    for c in range(tm // CHUNK):
        dtt_ref[c] = dtt[:, c * CHUNK:(c + 1) * CHUNK]
    dtt_pad = jnp.concatenate([dtt, jnp.zeros((LANES - dtt.shape[0], tm), F32)], axis=0)
    dt_ref[...] = dtt_pad.T

    qkv = dot(xb, wqkv_ref[...])
    n_slab = 3 * D_ATTN // LANES
    per = D_ATTN // LANES
    for s in range(n_slab):
        col = qkv[:, s * LANES:(s + 1) * LANES]
        if s < per:
            col = col * (HEAD_DIM ** -0.5 * LOG2E)
        slab_ref[s] = col

    z_ref[...] = dot(xb, wz_ref[...]).astype(z_ref.dtype)
    xbc_ref[...] = dot(xb, wxbc_ref[...]).astype(xbc_ref.dtype)
    nat = (q1_ref, k1_ref, v1_ref)
    d4 = (q4_ref, k4_ref, v4_ref)
    d16 = (q16_ref, k16_ref, v16_ref)
    n4, n16 = tm // 4, tm // 16
    for s in range(n_slab):
        a, c = divmod(s, per)
        lanes = slice(c * LANES, (c + 1) * LANES)
        nat[a][:, lanes] = slab_ref[s].astype(BF16)
        for r in range(4):
            part = slab_ref[s, pl.ds(r, n4, stride=4), :]
            d4[a][r, :, lanes] = part.astype(BF16)
            slab4_ref[s, r * n4:(r + 1) * n4, :] = part
        for r in range(4):
            for hi in range(4):
                part = slab4_ref[s, pl.ds(r * n4 + hi, n16, stride=4), :]
                d16[a][r + 4 * hi, :, lanes] = part.astype(BF16)


def _in_proj(x, wqkv, wz, wxbc, wdtt, dtbt):
    nseq, seq, _ = x.shape
    tm = min(TM_PROJ, seq)
    grid = (nseq, seq // tm)
    row = lambda w: pl.BlockSpec((None, tm, w), lambda n, j: (n, j, 0))
    res = lambda d: pl.BlockSpec((None, d, tm // d, D_ATTN), lambda n, j: (n, 0, j, 0))
    sds = jax.ShapeDtypeStruct
    out_shape = (
        [sds((nseq, seq, D_ATTN), BF16)] * 3
        + [sds((nseq, 4, seq // 4, D_ATTN), BF16)] * 3
        + [sds((nseq, 16, seq // 16, D_ATTN), BF16)] * 3
        + [sds((nseq, seq, D_SSD), BF16), sds((nseq, seq, D_XBC), BF16), sds((nseq, seq, LANES), F32),
           sds((nseq, seq // CHUNK, 16, CHUNK), F32)]
    )
    out_specs = (
        [row(D_ATTN)] * 3 + [res(4)] * 3 + [res(16)] * 3
        + [row(D_SSD), row(D_XBC), row(LANES),
           pl.BlockSpec((None, tm // CHUNK, 16, CHUNK), lambda n, j: (n, j, 0, 0))]
    )
    in_specs = [row(D_MODEL)] + [_const_spec(w.shape) for w in (wqkv, wz, wxbc, wdtt, dtbt)]
    return pl.pallas_call(
        functools.partial(_in_proj_kernel, tm=tm),
        grid=grid, in_specs=in_specs, out_specs=out_specs, out_shape=out_shape,
        scratch_shapes=[pltpu.VMEM((3 * D_ATTN // LANES, tm, LANES), F32)] * 2,
        compiler_params=_params(2), name="in_proj",
    )(x, wqkv, wz, wxbc, wdtt, dtbt)


def _conv_silu_chunk(ext_ref, shift_ref, convw_ref, convb_ref, c):
    span = CHUNK + 2 * HALO_BLK
    halves = []
    for half in range(D_XBC // 512):
        lanes = slice(half * 512, (half + 1) * 512)
        xe = ext_ref[c * CHUNK:c * CHUNK + span, lanes]
        acc = jnp.broadcast_to(convb_ref[:, lanes], (CHUNK, 512))
        for k in range(SSD_CONV):
            if k == SSD_CONV // 2:
                xk = xe[HALO_BLK:HALO_BLK + CHUNK].astype(F32)
            else:
                xk = jnp.dot(shift_ref[k], xe, preferred_element_type=F32)
            acc = acc + convw_ref[k:k + 1, lanes] * xk
        halves.append(acc * _sigmoid(acc))
    return jnp.concatenate(halves, axis=1)


def _ssd_prelude(xs, bm32, cm32, dt, dtt, a_row, a_col, tcol, trow, *, backward):
    head0 = N_HEADS if backward else 0
    tot_row = 0 if backward else CHUNK - 1
    qi = lax.broadcasted_iota(jnp.int32, (CHUNK, CHUNK), 0)
    si = lax.broadcasted_iota(jnp.int32, (CHUNK, CHUNK), 1)
    causal = (si >= qi) if backward else (si <= qi)
    lt64 = _lane_lt64((CHUNK, LANES))
    nt = (((1,), (1,)), ((), ()))
    acum = _exact_dot_r(tcol, dt * a_row)
    acum_t = _exact_dot_l(dtt * a_col, trow)
    bm = [b.astype(BF16) for b in bm32]
    cm = [c.astype(BF16) for c in cm32]
    cb = [lax.dot_general(cm[g], bm[g], nt, preferred_element_type=F32) for g in range(SSD_GROUPS)]
    pairs = []
    for p in range(N_HEADS // 2):
        g = p // 2
        cols = []
        for h in (2 * p, 2 * p + 1):
            lane = head0 + h
            cols.append((jnp.broadcast_to(acum[:, lane:lane + 1], (CHUNK, LANES)),
                         jnp.broadcast_to(dt[:, lane:lane + 1], (CHUNK, LANES))))
        a64 = jnp.where(lt64, cols[0][0], cols[1][0])
        dt64 = jnp.where(lt64, cols[0][1], cols[1][1])
        tot = a64[tot_row:tot_row + 1, :]
        xdt = xs[:, p * LANES:(p + 1) * LANES] * dt64
        lmats = []
        for i, h in enumerate((2 * p, 2 * p + 1)):
            arow = jnp.broadcast_to(acum_t[head0 + h:head0 + h + 1, :], (CHUNK, CHUNK))
            decay = jnp.exp2(jnp.where(causal, cols[i][0] - arow, -jnp.inf))
            lmats.append((cb[g] * decay).astype(BF16))
        pairs.append((lmats, xdt.astype(BF16), (xdt * jnp.exp2(tot - a64)).astype(BF16), jnp.exp2(tot),
                      jnp.exp2(a64)))
    return xs, bm, cm, pairs


def _ssd_body(pre, state_ref):
    xs, bm, cm, pairs = pre
    lt64 = _lane_lt64((CHUNK, LANES))
    dot = functools.partial(jnp.dot, preferred_element_type=F32)
    y_off = [dot(cm[g], state_ref[g].astype(BF16)) for g in range(SSD_GROUPS)]
    y_tiles = []
    for p, (lmats, xdt_b, _, _, ea64) in enumerate(pairs):
        ys = [dot(lm, xdt_b) for lm in lmats]
        off = y_off[p // 2][:, (p % 2) * LANES:(p % 2 + 1) * LANES]
        y_tiles.append(jnp.where(lt64, ys[0], ys[1]) + off * ea64)
    for g in range(SSD_GROUPS):
        xw = jnp.concatenate([pairs[2 * g][2], pairs[2 * g + 1][2]], axis=1)
        cd = jnp.concatenate([pairs[2 * g][3], pairs[2 * g + 1][3]], axis=1)
        upd = lax.dot_general(bm[g], xw, (((0,), (0,)), ((), ())), preferred_element_type=F32)
        state_ref[g] = cd * state_ref[g] + upd
    return jnp.concatenate(y_tiles, axis=1)


def _ssd_fwd_kernel(xprev_ref, xcur_ref, xnext_ref, dt_ref, dtt_ref, shift_ref, convw_ref, convb_ref,
                    alog_ref, alogt_ref, tcol_ref, trow_ref, dskip_ref,
                    act_ref, yf_ref, ext_ref, state_ref, *, rows, nblk):
    j = pl.program_id(1)

    @pl.when(j == 0)
    def _():
        state_ref[...] = jnp.zeros_like(state_ref)

    keep_prev = jnp.where(j == 0, 0.0, 1.0)
    keep_next = jnp.where(j == nblk - 1, 0.0, 1.0)
    ext_ref[0:HALO_BLK, :] = xprev_ref[...] * keep_prev.astype(BF16)
    ext_ref[HALO_BLK:HALO_BLK + rows, :] = xcur_ref[...]
    ext_ref[HALO_BLK + rows:2 * HALO_BLK + rows, :] = xnext_ref[...] * keep_next.astype(BF16)

    a_row = -jnp.exp(alog_ref[...]) * LOG2E
    a_col = -jnp.exp(alogt_ref[:, 0:1]) * LOG2E

    def prelude(c):
        rs = slice(c * CHUNK, (c + 1) * CHUNK)
        act = _conv_silu_chunk(ext_ref, shift_ref, convw_ref, convb_ref, c)
        act_ref[rs, :] = act.astype(act_ref.dtype)
        xs = act[:, 0:D_SSD]
        bm = [act[:, D_SSD + g * SSD_STATE:D_SSD + (g + 1) * SSD_STATE] for g in range(SSD_GROUPS)]
        cm = [act[:, D_SSD + (SSD_GROUPS + g) * SSD_STATE:D_SSD + (SSD_GROUPS + g + 1) * SSD_STATE]
              for g in range(SSD_GROUPS)]
        return _ssd_prelude(xs, bm, cm, dt_ref[rs, :], dtt_ref[c], a_row, a_col, tcol_ref[...], trow_ref[...],
                            backward=False)

    order = list(range(rows // CHUNK))
    pre = prelude(order[0])
    for i, c in enumerate(order):
        nxt = prelude(order[i + 1]) if i + 1 < len(order) else None
        y = _ssd_body(pre, state_ref)
        yf_ref[c * CHUNK:(c + 1) * CHUNK, :] = y + dskip_ref[...] * pre[0]
        pre = nxt


def _ssd_bwd_kernel(act_ref, dt_ref, dtt_ref, alog_ref, alogt_ref, tcol_ref, trow_ref,
                    yf_ref, z_ref, gnorm_ref, out_ref, state_ref, *, rows):
    j = pl.program_id(1)

    @pl.when(j == 0)
    def _():
        state_ref[...] = jnp.zeros_like(state_ref)

    a_row = -jnp.exp(alog_ref[...]) * LOG2E
    a_col = -jnp.exp(alogt_ref[:, 0:1]) * LOG2E

    def prelude(c):
        rs = slice(c * CHUNK, (c + 1) * CHUNK)
        xs = act_ref[rs, 0:D_SSD].astype(F32)
        bm = [act_ref[rs, D_SSD + g * SSD_STATE:D_SSD + (g + 1) * SSD_STATE] for g in range(SSD_GROUPS)]
        cm = [act_ref[rs, D_SSD + (SSD_GROUPS + g) * SSD_STATE:D_SSD + (SSD_GROUPS + g + 1) * SSD_STATE]
              for g in range(SSD_GROUPS)]
        return _ssd_prelude(xs, bm, cm, dt_ref[rs, :], dtt_ref[c], a_row, a_col, tcol_ref[...], trow_ref[...],
                            backward=True)

    order = list(reversed(range(rows // CHUNK)))
    pre = prelude(order[0])
    for i, c in enumerate(order):
        rs = slice(c * CHUNK, (c + 1) * CHUNK)
        nxt = prelude(order[i + 1]) if i + 1 < len(order) else None
        y = _ssd_body(pre, state_ref)
        zz = z_ref[rs, :].astype(F32)
        tot = (yf_ref[rs, :] + y) * (zz * _sigmoid(zz))
        inv = lax.rsqrt(jnp.mean(tot * tot, axis=-1, keepdims=True) + EPS)
        out_ref[rs, :] = (tot * inv * gnorm_ref[...]).astype(out_ref.dtype)
        pre = nxt


def _ssd(xbc, dt, dtt, z, consts_fwd, consts_bwd, dskip, gnorm):
    nseq, seq, _ = xbc.shape
    rows = min(R_SSD, seq)
    nblk = seq // rows
    hb = rows // HALO_BLK
    nh = seq // HALO_BLK
    state = pltpu.VMEM((SSD_GROUPS, SSD_STATE, 2 * LANES), F32)

    def specs(blk):
        cur = lambda w: pl.BlockSpec((None, rows, w), lambda n, j: (n, blk(j), 0))
        dtt_spec = pl.BlockSpec((None, rows // CHUNK, 16, CHUNK), lambda n, j: (n, blk(j), 0, 0))
        return cur, dtt_spec

    cur, dtt_spec = specs(lambda j: j)
    act, y_fwd = pl.pallas_call(
        functools.partial(_ssd_fwd_kernel, rows=rows, nblk=nblk),
        grid=(nseq, nblk),
        in_specs=[pl.BlockSpec((None, HALO_BLK, D_XBC), lambda n, j: (n, jnp.maximum(j * hb - 1, 0), 0)),
                  cur(D_XBC),
                  pl.BlockSpec((None, HALO_BLK, D_XBC), lambda n, j: (n, jnp.minimum((j + 1) * hb, nh - 1), 0)),
                  cur(LANES), dtt_spec]
        + [_const_spec(c.shape) for c in consts_fwd] + [_const_spec(dskip.shape)],
        out_specs=[cur(D_XBC), cur(D_SSD)],
        out_shape=[jax.ShapeDtypeStruct((nseq, seq, D_XBC), BF16),
                   jax.ShapeDtypeStruct((nseq, seq, D_SSD), F32)],
        scratch_shapes=[pltpu.VMEM((rows + 2 * HALO_BLK, D_XBC), BF16), state],
        compiler_params=_params(2), name="ssd_fwd",
    )(xbc, xbc, xbc, dt, dtt, *consts_fwd, dskip)

    cur, dtt_spec = specs(lambda j: nblk - 1 - j)
    return pl.pallas_call(
        functools.partial(_ssd_bwd_kernel, rows=rows),
        grid=(nseq, nblk),
        in_specs=[cur(D_XBC), cur(LANES), dtt_spec] + [_const_spec(c.shape) for c in consts_bwd]
        + [cur(D_SSD), cur(D_SSD), _const_spec(gnorm.shape)],
        out_specs=cur(D_SSD),
        out_shape=jax.ShapeDtypeStruct((nseq, seq, D_SSD), BF16),
        scratch_shapes=[state],
        compiler_params=_params(2), name="ssd_bwd",
    )(act, dt, dtt, *consts_bwd, y_fwd, z, gnorm)


def _attn_kernel(q_ref, kprev_ref, kcur_ref, knext_ref, vprev_ref, vcur_ref, vnext_ref, bias_ref,
                 o_ref, lse_ref, kbuf_ref, vbuf_ref, *, rows, nblk, gb):
    j = pl.program_id(1)
    first = j == 0
    last = j == nblk - 1
    for buf, prev, cur, nxt in ((kbuf_ref, kprev_ref, kcur_ref, knext_ref),
                                (vbuf_ref, vprev_ref, vcur_ref, vnext_ref)):
        buf[:, 0:SIDE, :] = prev[...]
        buf[:, SIDE:SIDE + rows, :] = cur[...]
        buf[:, SIDE + rows:2 * SIDE + rows, :] = nxt[...]
    win = QB + 2 * SIDE
    lt64 = _lane_lt64((QB, LANES))
    lane = lax.broadcasted_iota(jnp.int32, (QB, LANES), 1)
    col = lax.broadcasted_iota(jnp.int32, (2 * QB, win), 1)
    nqb = rows // QB
    blocks = [(g, qb, p) for g in range(gb) for qb in range(nqb) for p in range(N_HEADS // 2)]

    def logits(g, qb, p):
        lanes = slice(p * LANES, (p + 1) * LANES)
        qp = q_ref[g, qb * QB:(qb + 1) * QB, lanes]
        zero = jnp.zeros_like(qp)
        q2 = jnp.concatenate([jnp.where(lt64, qp, zero), jnp.where(lt64, zero, qp)], axis=0)
        s = lax.dot_general(q2, kbuf_ref[g, qb * QB:qb * QB + win, lanes], (((1,), (1,)), ((), ())),
                            preferred_element_type=F32) + bias_ref[p]
        if qb == 0:
            s = jnp.where(jnp.logical_and(col < SIDE, first), NEG, s)
        if qb == nqb - 1:
            s = jnp.where(jnp.logical_and(col >= win - SIDE, last), NEG, s)
        return s

    pending = [logits(*blocks[i]) for i in range(min(ATT_PIPE, len(blocks)))]
    lse_pk = None
    for i, (g, qb, p) in enumerate(blocks):
        s = pending.pop(0)
        if i + ATT_PIPE < len(blocks):
            pending.append(logits(*blocks[i + ATT_PIPE]))
        lanes = slice(p * LANES, (p + 1) * LANES)
        qrows = slice(qb * QB, (qb + 1) * QB)
        m = jnp.max(s, axis=1, keepdims=True)
        e = jnp.exp2(s - m)
        l = jnp.sum(e, axis=1, keepdims=True)
        o2 = jnp.dot(e.astype(BF16), vbuf_ref[g, qb * QB:qb * QB + win, lanes], preferred_element_type=F32)
        denom = jnp.where(lt64, l[0:QB], l[QB:2 * QB])
        o_ref[g, qrows, lanes] = jnp.where(lt64, o2[0:QB], o2[QB:2 * QB]) / denom
        lse2 = jnp.where(lt64, m[0:QB], m[QB:2 * QB]) + jnp.log2(denom)
        part = jnp.where(jnp.logical_or(lane == _lse_lane(2 * p), lane == _lse_lane(2 * p + 1)), lse2, 0.0)
        lse_pk = part if p == 0 else lse_pk + part
        if p == N_HEADS // 2 - 1:
            lse_ref[g, qrows, :] = lse_pk


def _attn_branch(q, k, v, bias):
    ng, length, _ = q.shape
    rows = min(RB_ATT, length)
    nblk = length // rows
    gb = math.gcd(ng, max(1, RB_ATT // length))
    hb = rows // SIDE
    nh = length // SIDE
    cur = lambda w: pl.BlockSpec((gb, rows, w), lambda g, j: (g, j, 0))
    prev = pl.BlockSpec((gb, SIDE, D_ATTN), lambda g, j: (g, jnp.maximum(j * hb - 1, 0), 0))
    nxt = pl.BlockSpec((gb, SIDE, D_ATTN), lambda g, j: (g, jnp.minimum((j + 1) * hb, nh - 1), 0))
    return pl.pallas_call(
        functools.partial(_attn_kernel, rows=rows, nblk=nblk, gb=gb),
        grid=(ng // gb, nblk),
        in_specs=[cur(D_ATTN), prev, cur(D_ATTN), nxt, prev, cur(D_ATTN), nxt, _const_spec(bias.shape)],
        out_specs=[cur(D_ATTN), cur(LANES)],
        out_shape=[jax.ShapeDtypeStruct((ng, length, D_ATTN), F32),
                   jax.ShapeDtypeStruct((ng, length, LANES), F32)],
        scratch_shapes=[pltpu.VMEM((gb, rows + 2 * SIDE, D_ATTN), BF16),
                        pltpu.VMEM((gb, rows + 2 * SIDE, D_ATTN), BF16)],
        compiler_params=_params(2), name="dilated_attn",
    )(q, k, k, k, v, v, v, bias)


def _merge_tile(o1_ref, o4_ref, o16_ref, l1_ref, l4_ref, l16_ref, gnorm_ref, dst_ref,
                so4_ref, so16_ref, sl4_ref, sl16_ref, *, tm):
    per = D_ATTN // LANES
    for d, o_ref, l_ref, so_ref, sl_ref in ((4, o4_ref, l4_ref, so4_ref, sl4_ref),
                                            (16, o16_ref, l16_ref, so16_ref, sl16_ref)):
        for r in range(d):
            rows = pl.ds(r, tm // d, stride=d)
            sl_ref[rows, :] = l_ref[r]
            for s in range(per):
                so_ref[s, rows, :] = o_ref[r, :, s * LANES:(s + 1) * LANES]
    lt64 = _lane_lt64((QB, LANES))
    for b in range(tm // QB):
        rs = slice(b * QB, (b + 1) * QB)
        l1, l4, l16 = l1_ref[rs, :], sl4_ref[rs, :], sl16_ref[rs, :]
        m = jnp.maximum(jnp.maximum(l1, l4), l16)
        e1, e4, e16 = jnp.exp2(l1 - m), jnp.exp2(l4 - m), jnp.exp2(l16 - m)
        inv = 1.0 / (e1 + e4 + e16)
        w4, w16 = e4 * inv, e16 * inv
        tiles = []
        for p in range(per):
            lanes = slice(p * LANES, (p + 1) * LANES)
            c0, c1 = _lse_lane(2 * p), _lse_lane(2 * p + 1)
            w4p = jnp.where(lt64, jnp.broadcast_to(w4[:, c0:c0 + 1], (QB, LANES)),
                            jnp.broadcast_to(w4[:, c1:c1 + 1], (QB, LANES)))
            w16p = jnp.where(lt64, jnp.broadcast_to(w16[:, c0:c0 + 1], (QB, LANES)),
                             jnp.broadcast_to(w16[:, c1:c1 + 1], (QB, LANES)))
            o1 = o1_ref[rs, lanes]
            tiles.append(o1 + w4p * (so4_ref[p, rs, :] - o1) + w16p * (so16_ref[p, rs, :] - o1))
        o = jnp.concatenate(tiles, axis=1)
        inv_rms = lax.rsqrt(jnp.mean(o * o, axis=-1, keepdims=True) + EPS)
        dst_ref[rs, :] = (o * inv_rms * gnorm_ref[...]).astype(dst_ref.dtype)


def _layer_norm(x, g, b):
    mu = jnp.mean(x, axis=-1, keepdims=True)
    xc = x - mu
    var = jnp.mean(xc * xc, axis=-1, keepdims=True)
    return xc * lax.rsqrt(var + EPS) * g + b


def _tail_kernel(x_ref, y_ref, o1_ref, o4_ref, o16_ref, l1_ref, l4_ref, l16_ref, gattn_ref,
                 woa_ref, woy_ref, g1_ref, b1_ref, wg_ref, wu_ref, wd_ref, g2_ref, b2_ref,
                 out_ref, attn_ref, so4_ref, so16_ref, sl4_ref, sl16_ref, *, tm):
    t = pl.program_id(0)

    @pl.when(t == 0)
    def _():
        attn_ref[1] = jnp.zeros(attn_ref.shape[1:], attn_ref.dtype)

    dot = functools.partial(jnp.dot, preferred_element_type=F32)
    attn = attn_ref[(t + 1) % 2]
    mix = dot(attn, woa_ref[...]) + dot(y_ref[...], woy_ref[...])
    h1 = _layer_norm(ALPHA * x_ref[...] + mix, g1_ref[...], b1_ref[...])
    h1b = h1.astype(BF16)
    ffn = jnp.zeros_like(h1)
    for c in range(D_FF // FF_CHUNK):
        cols = slice(c * FF_CHUNK, (c + 1) * FF_CHUNK)
        gate = dot(h1b, wg_ref[:, cols])
        up = dot(h1b, wu_ref[:, cols])
        hid = (gate * _sigmoid(gate) * up).astype(BF16)
        ffn = ffn + dot(hid, wd_ref[cols, :])
    out_ref[...] = _layer_norm(ALPHA * h1 + ffn, g2_ref[...], b2_ref[...])

    _merge_tile(o1_ref, o4_ref, o16_ref, l1_ref, l4_ref, l16_ref, gattn_ref, attn_ref.at[t % 2],
                so4_ref, so16_ref, sl4_ref, sl16_ref, tm=tm)


def _tail(x, y, o1, o4, o16, l1, l4, l16, gattn, woa, woy, g1, b1, wg, wu, wd, g2, b2):
    nseq, seq, _ = x.shape
    tm = min(TM_FFN, seq)
    nblk = seq // tm
    ntile = nseq * nblk
    ffn_tile = lambda t: jnp.maximum(t - 1, 0)
    mrg_tile = lambda t: jnp.minimum(t, ntile - 1)
    row = lambda w, tile: pl.BlockSpec((None, tm, w), lambda t: (tile(t) // nblk, tile(t) % nblk, 0))
    res = lambda d, w: pl.BlockSpec((None, d, tm // d, w),
                                    lambda t: (mrg_tile(t) // nblk, 0, mrg_tile(t) % nblk, 0))
    consts = (gattn, woa, woy, g1, b1, wg, wu, wd, g2, b2)
    per = D_ATTN // LANES
    return pl.pallas_call(
        functools.partial(_tail_kernel, tm=tm),
        grid=(ntile + 1,),
        in_specs=[row(D_MODEL, ffn_tile), row(D_SSD, ffn_tile),
                  row(D_ATTN, mrg_tile), res(4, D_ATTN), res(16, D_ATTN),
                  row(LANES, mrg_tile), res(4, LANES), res(16, LANES)]
        + [_const_spec(c.shape) for c in consts],
        out_specs=row(D_MODEL, ffn_tile),
        out_shape=jax.ShapeDtypeStruct((nseq, seq, D_MODEL), F32),
        scratch_shapes=[pltpu.VMEM((2, tm, D_ATTN), BF16),
                        pltpu.VMEM((per, tm, LANES), F32), pltpu.VMEM((per, tm, LANES), F32),
                        pltpu.VMEM((tm, LANES), F32), pltpu.VMEM((tm, LANES), F32)],
        compiler_params=_params(1), name="merge_ffn",
    )(x, y, o1, o4, o16, l1, l4, l16, *consts)


def _t5_bucket(rel):
    half = REL_BUCKETS // 2
    max_exact = half // 2
    ret = (rel > 0).astype(np.int32) * half
    n = np.abs(rel)
    large = max_exact + (np.log(np.maximum(n, 1) / max_exact)
                         / math.log(REL_MAX_DIST / max_exact) * (half - max_exact)).astype(np.int32)
    large = np.minimum(large, half - 1)
    return ret + np.where(n < max_exact, n, large)


def _bias_tables(rel_bias):
    win = QB + 2 * SIDE
    period = win + QB
    tables = []
    for d in DILATIONS:
        buckets = _t5_bucket((np.arange(2 * SIDE + 1) - SIDE) * d)
        band = rel_bias.astype(F32)[buckets].T * LOG2E
        vec = jnp.pad(band, ((0, 0), (0, period - band.shape[1])), constant_values=NEG)
        flat = jnp.tile(vec, (1, QB))[:, :QB * (period - 1)]
        toe = flat.reshape(N_HEADS, QB, period - 1)[:, :, :win]
        tables.append(toe.reshape(N_HEADS // 2, 2 * QB, win))
    return tables


def _selector_constants():
    tri = np.tril(np.ones((CHUNK, CHUNK), np.float32))
    to = lambda a: jnp.asarray(a, BF16)
    return to(tri), to(tri.T)


def _pad_lanes(v, width=LANES):
    return jnp.pad(v, ((0, 0), (0, width - v.shape[1])))


def _prepare(rel_bias, w_in, conv_w, conv_b, dt_bias_fwd, dt_bias_bwd, a_log_fwd, a_log_bwd, d_skip,
             attn_norm_g, ssd_norm_g, w_out, ln1_g, ln1_b, w_gate, w_up, w_down, ln2_g, ln2_b):
    w = w_in[0]
    o = 3 * D_ATTN
    p = {}
    p["wqkv"] = w[:, :o].astype(BF16)
    p["wz"] = w[:, o:o + D_SSD].astype(BF16)
    p["wxbc"] = w[:, o + D_SSD:o + D_SSD + D_XBC].astype(BF16)
    wdt = w[:, o + D_SSD + D_XBC:].astype(BF16)
    p["wdtt"] = wdt.T
    dtb = jnp.concatenate([dt_bias_fwd[0], dt_bias_bwd[0]]).astype(F32)[None, :]
    p["dtbt"] = jnp.broadcast_to(dtb.T, (2 * N_HEADS, LANES))
    alog = jnp.concatenate([a_log_fwd[0], a_log_bwd[0]]).astype(F32)[None, :]
    tri, tri_t = _selector_constants()
    convw = jnp.pad(conv_w[0].astype(F32), ((0, 8 - SSD_CONV), (0, 0)))
    a_tabs = (_pad_lanes(alog), jnp.broadcast_to(alog.T, (2 * N_HEADS, LANES)))
    shifts = np.zeros((SSD_CONV, CHUNK, CHUNK + 2 * HALO_BLK), np.float32)
    for k in range(SSD_CONV):
        shifts[k, np.arange(CHUNK), np.arange(CHUNK) + HALO_BLK + k - SSD_CONV // 2] = 1.0
    p["ssd_fwd"] = (jnp.asarray(shifts, BF16), convw, conv_b[0].astype(F32)[None, :]) + a_tabs + (tri, tri_t)
    p["ssd_bwd"] = a_tabs + (tri_t, tri)
    p["dskip"] = jnp.repeat(d_skip[0].astype(F32), HEAD_DIM)[None, :]
    p["ssd_g"] = ssd_norm_g[0].astype(F32)[None, :]
    p["attn_g"] = attn_norm_g[0].astype(F32)[None, :]
    p["bias"] = _bias_tables(rel_bias)
    wo = w_out[0].astype(BF16)
    p["tail"] = (p["attn_g"], wo[:D_ATTN], wo[D_ATTN:], ln1_g[0].astype(F32)[None, :], ln1_b[0].astype(F32)[None, :],
                w_gate[0].astype(BF16), w_up[0].astype(BF16), w_down[0].astype(BF16),
                ln2_g[0].astype(F32)[None, :], ln2_b[0].astype(F32)[None, :])
    return p


def _encoder_layer(x, p):
    nseq, seq, _ = x.shape
    (q1, k1, v1, q4, k4, v4, q16, k16, v16, z, xbc, dt, dtt) = _in_proj(
        x, p["wqkv"], p["wz"], p["wxbc"], p["wdtt"], p["dtbt"])

    y_ssd = _ssd(xbc, dt, dtt, z, p["ssd_fwd"], p["ssd_bwd"], p["dskip"], p["ssd_g"])

    outs, lses = [], []
    for d, (q, k, v), bias in zip(DILATIONS, ((q1, k1, v1), (q4, k4, v4), (q16, k16, v16)), p["bias"]):
        flat = lambda t: t.reshape(nseq * d, seq // d, D_ATTN)
        o, lse = _attn_branch(flat(q), flat(k), flat(v), bias)
        if d > 1:
            o = o.reshape(nseq, d, seq // d, D_ATTN)
            lse = lse.reshape(nseq, d, seq // d, LANES)
        outs.append(o)
        lses.append(lse)
    return _tail(x, y_ssd, *outs, *lses, *p["tail"])


def kernel(x_prompt, x_sample, rel_bias, w_in, conv_w, conv_b, dt_bias_fwd, dt_bias_bwd, a_log_fwd,
           a_log_bwd, d_skip, attn_norm_g, ssd_norm_g, w_out, ln1_g, ln1_b, w_gate, w_up, w_down,
           ln2_g, ln2_b):
    p = _prepare(rel_bias, w_in, conv_w, conv_b, dt_bias_fwd, dt_bias_bwd, a_log_fwd, a_log_bwd, d_skip,
                 attn_norm_g, ssd_norm_g, w_out, ln1_g, ln1_b, w_gate, w_up, w_down, ln2_g, ln2_b)
    return (_encoder_layer(x_prompt, p), _encoder_layer(x_sample, p))
```

```python
import functools
import math

import jax
import jax.numpy as jnp
import numpy as np
from jax import lax
from jax.experimental import pallas as pl
from jax.experimental.pallas import tpu as pltpu

F32 = jnp.float32
BF16 = jnp.bfloat16

D_MODEL = 1024
D_ATTN = 512
HEAD_DIM = 64
N_HEADS = 8
D_SSD = 512
SSD_GROUPS = 2
SSD_STATE = 128
SSD_CONV = 5
CHUNK = 128
D_FF = 2816
D_XBC = D_SSD + 2 * SSD_GROUPS * SSD_STATE
DILATIONS = (1, 4, 16)
SIDE = 64
REL_BUCKETS = 32
REL_MAX_DIST = 1024
DEPTH = 1
ALPHA = (2 * DEPTH) ** 0.25
EPS = 1e-5
NEG = -1e30
LOG2E = 1.4426950408889634

LANES = 128
SUBLANES = 8
HALO_BLK = 16
DT_ROWS = 2 * N_HEADS
CONV_LANES = 512
VMEM_LIMIT = 56 * 1024 * 1024

TM_PROJ = 1024
R_SSD = 512
RB_ATT = 2048
QB = 128
ATT_PIPE = 1
TM_FFN = 512
FF_CHUNK = 256


def _const_spec(shape):
    nd = len(shape)
    return pl.BlockSpec(shape, lambda *_: (0,) * nd, pipeline_mode=pl.Buffered(1))


def _params(n_axes):
    return pltpu.CompilerParams(dimension_semantics=("arbitrary",) * n_axes,
                                vmem_limit_bytes=VMEM_LIMIT)


def _sigmoid(x):
    return 1.0 / (1.0 + jnp.exp(-x))


def _softplus(x):
    return jnp.maximum(x, 0.0) + jnp.log1p(jnp.exp(-jnp.abs(x)))


def _split3(x):
    hi = x.astype(BF16)
    r1 = x - hi.astype(F32)
    mid = r1.astype(BF16)
    lo = (r1 - mid.astype(F32)).astype(BF16)
    return hi, mid, lo


def _exact_dot_r(sel, x):
    hi, mid, lo = _split3(x)
    d = functools.partial(jnp.dot, preferred_element_type=F32)
    return d(sel, hi) + d(sel, mid) + d(sel, lo)


def _exact_dot_l(x, sel):
    hi, mid, lo = _split3(x)
    d = functools.partial(jnp.dot, preferred_element_type=F32)
    return d(hi, sel) + d(mid, sel) + d(lo, sel)


def _lane_lt64(shape):
    return lax.broadcasted_iota(jnp.int32, shape, len(shape) - 1) < HEAD_DIM


def _lse_lane(h):
    return h if h % 2 == 0 else HEAD_DIM + h


def _in_proj_kernel(x_ref, wqkv_ref, wz_ref, wxbc_ref, wdtt_ref, dtbt_ref,
                    q1_ref, k1_ref, v1_ref, q4_ref, k4_ref, v4_ref, q16_ref, k16_ref, v16_ref,
                    z_ref, xbc_ref, dt_ref, dtt_ref, slab_ref, slab4_ref, *, tm):
    xb = x_ref[...].astype(BF16)
    dot = functools.partial(jnp.dot, preferred_element_type=F32)

    dtt = lax.dot_general(wdtt_ref[...], xb, (((1,), (1,)), ((), ())), preferred_element_type=F32)
    dtt = _softplus(dtt + dtbt_ref[:, 0:1])
    for c in range(tm // CHUNK):
        dtt_ref[c] = dtt[:, c * CHUNK:(c + 1) * CHUNK]
    dtt_pad = jnp.concatenate([dtt, jnp.zeros((LANES - dtt.shape[0], tm), F32)], axis=0)
    dt_ref[...] = dtt_pad.T

    qkv = dot(xb, wqkv_ref[...])
    n_slab = 3 * D_ATTN // LANES
    per = D_ATTN // LANES
    for s in range(n_slab):
        col = qkv[:, s * LANES:(s + 1) * LANES]
        if s < per:
            col = col * (HEAD_DIM ** -0.5 * LOG2E)
        slab_ref[s] = col

    z_ref[...] = dot(xb, wz_ref[...]).astype(z_ref.dtype)
    xbc_ref[...] = dot(xb, wxbc_ref[...]).astype(xbc_ref.dtype)
    nat = (q1_ref, k1_ref, v1_ref)
    d4 = (q4_ref, k4_ref, v4_ref)
    d16 = (q16_ref, k16_ref, v16_ref)
    n4, n16 = tm // 4, tm // 16
    for s in range(n_slab):
        a, c = divmod(s, per)
        lanes = slice(c * LANES, (c + 1) * LANES)
        nat[a][:, lanes] = slab_ref[s].astype(BF16)
        for r in range(4):
            part = slab_ref[s, pl.ds(r, n4, stride=4), :]
            d4[a][r, :, lanes] = part.astype(BF16)
            slab4_ref[s, r * n4:(r + 1) * n4, :] = part
        for r in range(4):
            for hi in range(4):
                part = slab4_ref[s, pl.ds(r * n4 + hi, n16, stride=4), :]
                d16[a][r + 4 * hi, :, lanes] = part.astype(BF16)


def _in_proj(x, wqkv, wz, wxbc, wdtt, dtbt):
    nseq, seq, _ = x.shape
    tm = min(TM_PROJ, seq)
    grid = (nseq, seq // tm)
    row = lambda w: pl.BlockSpec((None, tm, w), lambda n, j: (n, j, 0))
    res = lambda d: pl.BlockSpec((None, d, tm // d, D_ATTN), lambda n, j: (n, 0, j, 0))
    sds = jax.ShapeDtypeStruct
    out_shape = (
        [sds((nseq, seq, D_ATTN), BF16)] * 3
        + [sds((nseq, 4, seq // 4, D_ATTN), BF16)] * 3
        + [sds((nseq, 16, seq // 16, D_ATTN), BF16)] * 3
        + [sds((nseq, seq, D_SSD), BF16), sds((nseq, seq, D_XBC), BF16), sds((nseq, seq, LANES), F32),
           sds((nseq, seq // CHUNK, DT_ROWS, CHUNK), F32)]
    )
    out_specs = (
        [row(D_ATTN)] * 3 + [res(4)] * 3 + [res(16)] * 3
        + [row(D_SSD), row(D_XBC), row(LANES),
           pl.BlockSpec((None, tm // CHUNK, DT_ROWS, CHUNK), lambda n, j: (n, j, 0, 0))]
    )
    in_specs = [row(D_MODEL)] + [_const_spec(w.shape) for w in (wqkv, wz, wxbc, wdtt, dtbt)]
    return pl.pallas_call(
        functools.partial(_in_proj_kernel, tm=tm),
        grid=grid, in_specs=in_specs, out_specs=out_specs, out_shape=out_shape,
        scratch_shapes=[pltpu.VMEM((3 * D_ATTN // LANES, tm, LANES), F32)] * 2,
        compiler_params=_params(2), name="in_proj",
    )(x, wqkv, wz, wxbc, wdtt, dtbt)


def _conv_silu_chunk(ext_ref, shift_ref, convw_ref, convb_ref, c):
    span = CHUNK + 2 * HALO_BLK
    halves = []
    for part in range(D_XBC // CONV_LANES):
        lanes = slice(part * CONV_LANES, (part + 1) * CONV_LANES)
        xe = ext_ref[c * CHUNK:c * CHUNK + span, lanes]
        acc = jnp.broadcast_to(convb_ref[:, lanes], (CHUNK, CONV_LANES))
        for k in range(SSD_CONV):
            if k == SSD_CONV // 2:
                xk = xe[HALO_BLK:HALO_BLK + CHUNK].astype(F32)
            else:
                xk = jnp.dot(shift_ref[k], xe, preferred_element_type=F32)
            acc = acc + convw_ref[k:k + 1, lanes] * xk
        halves.append(acc * _sigmoid(acc))
    return jnp.concatenate(halves, axis=1)


def _ssd_prelude(xs, bm32, cm32, dt, dtt, a_row, a_col, tcol, trow, *, backward):
    head0 = N_HEADS if backward else 0
    tot_row = 0 if backward else CHUNK - 1
    qi = lax.broadcasted_iota(jnp.int32, (CHUNK, CHUNK), 0)
    si = lax.broadcasted_iota(jnp.int32, (CHUNK, CHUNK), 1)
    causal = (si >= qi) if backward else (si <= qi)
    lt64 = _lane_lt64((CHUNK, LANES))
    nt = (((1,), (1,)), ((), ()))
    acum = _exact_dot_r(tcol, dt * a_row)
    acum_t = _exact_dot_l(dtt * a_col, trow)
    bm = [b.astype(BF16) for b in bm32]
    cm = [c.astype(BF16) for c in cm32]
    cb = [lax.dot_general(cm[g], bm[g], nt, preferred_element_type=F32) for g in range(SSD_GROUPS)]
    pairs = []
    for p in range(N_HEADS // 2):
        g = p // 2
        cols = []
        for h in (2 * p, 2 * p + 1):
            lane = head0 + h
            cols.append((jnp.broadcast_to(acum[:, lane:lane + 1], (CHUNK, LANES)),
                         jnp.broadcast_to(dt[:, lane:lane + 1], (CHUNK, LANES))))
        a64 = jnp.where(lt64, cols[0][0], cols[1][0])
        dt64 = jnp.where(lt64, cols[0][1], cols[1][1])
        tot = a64[tot_row:tot_row + 1, :]
        xdt = xs[:, p * LANES:(p + 1) * LANES] * dt64
        lmats = []
        for i, h in enumerate((2 * p, 2 * p + 1)):
            arow = jnp.broadcast_to(acum_t[head0 + h:head0 + h + 1, :], (CHUNK, CHUNK))
            decay = jnp.exp2(jnp.where(causal, cols[i][0] - arow, -jnp.inf))
            lmats.append((cb[g] * decay).astype(BF16))
        pairs.append((lmats, xdt.astype(BF16), (xdt * jnp.exp2(tot - a64)).astype(BF16), jnp.exp2(tot),
                      jnp.exp2(a64)))
    return xs, bm, cm, pairs


def _ssd_body(pre, state_ref):
    xs, bm, cm, pairs = pre
    lt64 = _lane_lt64((CHUNK, LANES))
    dot = functools.partial(jnp.dot, preferred_element_type=F32)
    y_off = [dot(cm[g], state_ref[g].astype(BF16)) for g in range(SSD_GROUPS)]
    y_tiles = []
    for p, (lmats, xdt_b, _, _, ea64) in enumerate(pairs):
        ys = [dot(lm, xdt_b) for lm in lmats]
        off = y_off[p // 2][:, (p % 2) * LANES:(p % 2 + 1) * LANES]
        y_tiles.append(jnp.where(lt64, ys[0], ys[1]) + off * ea64)
    for g in range(SSD_GROUPS):
        xw = jnp.concatenate([pairs[2 * g][2], pairs[2 * g + 1][2]], axis=1)
        cd = jnp.concatenate([pairs[2 * g][3], pairs[2 * g + 1][3]], axis=1)
        upd = lax.dot_general(bm[g], xw, (((0,), (0,)), ((), ())), preferred_element_type=F32)
        state_ref[g] = cd * state_ref[g] + upd
    return jnp.concatenate(y_tiles, axis=1)


def _ssd_fwd_kernel(xprev_ref, xcur_ref, xnext_ref, dt_ref, dtt_ref, shift_ref, convw_ref, convb_ref,
                    alog_ref, alogt_ref, tcol_ref, trow_ref, dskip_ref,
                    act_ref, yf_ref, ext_ref, state_ref, *, rows, nblk):
    j = pl.program_id(1)

    @pl.when(j == 0)
    def _():
        state_ref[...] = jnp.zeros_like(state_ref)

    keep_prev = jnp.where(j == 0, 0.0, 1.0)
    keep_next = jnp.where(j == nblk - 1, 0.0, 1.0)
    ext_ref[0:HALO_BLK, :] = xprev_ref[...] * keep_prev.astype(BF16)
    ext_ref[HALO_BLK:HALO_BLK + rows, :] = xcur_ref[...]
    ext_ref[HALO_BLK + rows:2 * HALO_BLK + rows, :] = xnext_ref[...] * keep_next.astype(BF16)

    a_row = -jnp.exp(alog_ref[...]) * LOG2E
    a_col = -jnp.exp(alogt_ref[:, 0:1]) * LOG2E

    def prelude(c):
        rs = slice(c * CHUNK, (c + 1) * CHUNK)
        act = _conv_silu_chunk(ext_ref, shift_ref, convw_ref, convb_ref, c)
        act_ref[rs, :] = act.astype(act_ref.dtype)
        xs = act[:, 0:D_SSD]
        bm = [act[:, D_SSD + g * SSD_STATE:D_SSD + (g + 1) * SSD_STATE] for g in range(SSD_GROUPS)]
        cm = [act[:, D_SSD + (SSD_GROUPS + g) * SSD_STATE:D_SSD + (SSD_GROUPS + g + 1) * SSD_STATE]
              for g in range(SSD_GROUPS)]
        return _ssd_prelude(xs, bm, cm, dt_ref[rs, :], dtt_ref[c], a_row, a_col, tcol_ref[...], trow_ref[...],
                            backward=False)

    order = list(range(rows // CHUNK))
    pre = prelude(order[0])
    for i, c in enumerate(order):
        nxt = prelude(order[i + 1]) if i + 1 < len(order) else None
        y = _ssd_body(pre, state_ref)
        yf_ref[c * CHUNK:(c + 1) * CHUNK, :] = y + dskip_ref[...] * pre[0]
        pre = nxt


def _ssd_bwd_kernel(act_ref, dt_ref, dtt_ref, alog_ref, alogt_ref, tcol_ref, trow_ref,
                    yf_ref, z_ref, gnorm_ref, out_ref, state_ref, *, rows):
    j = pl.program_id(1)

    @pl.when(j == 0)
    def _():
        state_ref[...] = jnp.zeros_like(state_ref)

    a_row = -jnp.exp(alog_ref[...]) * LOG2E
    a_col = -jnp.exp(alogt_ref[:, 0:1]) * LOG2E

    def prelude(c):
        rs = slice(c * CHUNK, (c + 1) * CHUNK)
        xs = act_ref[rs, 0:D_SSD].astype(F32)
        bm = [act_ref[rs, D_SSD + g * SSD_STATE:D_SSD + (g + 1) * SSD_STATE] for g in range(SSD_GROUPS)]
        cm = [act_ref[rs, D_SSD + (SSD_GROUPS + g) * SSD_STATE:D_SSD + (SSD_GROUPS + g + 1) * SSD_STATE]
              for g in range(SSD_GROUPS)]
        return _ssd_prelude(xs, bm, cm, dt_ref[rs, :], dtt_ref[c], a_row, a_col, tcol_ref[...], trow_ref[...],
                            backward=True)

    order = list(reversed(range(rows // CHUNK)))
    pre = prelude(order[0])
    for i, c in enumerate(order):
        rs = slice(c * CHUNK, (c + 1) * CHUNK)
        nxt = prelude(order[i + 1]) if i + 1 < len(order) else None
        y = _ssd_body(pre, state_ref)
        zz = z_ref[rs, :].astype(F32)
        tot = (yf_ref[rs, :] + y) * (zz * _sigmoid(zz))
        inv = lax.rsqrt(jnp.mean(tot * tot, axis=-1, keepdims=True) + EPS)
        out_ref[rs, :] = (tot * inv * gnorm_ref[...]).astype(out_ref.dtype)
        pre = nxt


def _ssd(xbc, dt, dtt, z, consts_fwd, consts_bwd, dskip, gnorm):
    nseq, seq, _ = xbc.shape
    rows = min(R_SSD, seq)
    nblk = seq // rows
    hb = rows // HALO_BLK
    nh = seq // HALO_BLK
    state = pltpu.VMEM((SSD_GROUPS, SSD_STATE, 2 * LANES), F32)

    def specs(blk):
        cur = lambda w: pl.BlockSpec((None, rows, w), lambda n, j: (n, blk(j), 0))
        dtt_spec = pl.BlockSpec((None, rows // CHUNK, DT_ROWS, CHUNK), lambda n, j: (n, blk(j), 0, 0))
        return cur, dtt_spec

    cur, dtt_spec = specs(lambda j: j)
    act, y_fwd = pl.pallas_call(
        functools.partial(_ssd_fwd_kernel, rows=rows, nblk=nblk),
        grid=(nseq, nblk),
        in_specs=[pl.BlockSpec((None, HALO_BLK, D_XBC), lambda n, j: (n, jnp.maximum(j * hb - 1, 0), 0)),
                  cur(D_XBC),
                  pl.BlockSpec((None, HALO_BLK, D_XBC), lambda n, j: (n, jnp.minimum((j + 1) * hb, nh - 1), 0)),
                  cur(LANES), dtt_spec]
        + [_const_spec(c.shape) for c in consts_fwd] + [_const_spec(dskip.shape)],
        out_specs=[cur(D_XBC), cur(D_SSD)],
        out_shape=[jax.ShapeDtypeStruct((nseq, seq, D_XBC), BF16),
                   jax.ShapeDtypeStruct((nseq, seq, D_SSD), F32)],
        scratch_shapes=[pltpu.VMEM((rows + 2 * HALO_BLK, D_XBC), BF16), state],
        compiler_params=_params(2), name="ssd_fwd",
    )(xbc, xbc, xbc, dt, dtt, *consts_fwd, dskip)

    cur, dtt_spec = specs(lambda j: nblk - 1 - j)
    return pl.pallas_call(
        functools.partial(_ssd_bwd_kernel, rows=rows),
        grid=(nseq, nblk),
        in_specs=[cur(D_XBC), cur(LANES), dtt_spec] + [_const_spec(c.shape) for c in consts_bwd]
        + [cur(D_SSD), cur(D_SSD), _const_spec(gnorm.shape)],
        out_specs=cur(D_SSD),
        out_shape=jax.ShapeDtypeStruct((nseq, seq, D_SSD), BF16),
        scratch_shapes=[state],
        compiler_params=_params(2), name="ssd_bwd",
    )(act, dt, dtt, *consts_bwd, y_fwd, z, gnorm)


def _attn_kernel(q_ref, kprev_ref, kcur_ref, knext_ref, vprev_ref, vcur_ref, vnext_ref, bias_ref,
                 o_ref, lse_ref, kbuf_ref, vbuf_ref, *, rows, nblk, gb):
    j = pl.program_id(1)
    first = j == 0
    last = j == nblk - 1
    for buf, prev, cur, nxt in ((kbuf_ref, kprev_ref, kcur_ref, knext_ref),
                                (vbuf_ref, vprev_ref, vcur_ref, vnext_ref)):
        buf[:, 0:SIDE, :] = prev[...]
        buf[:, SIDE:SIDE + rows, :] = cur[...]
        buf[:, SIDE + rows:2 * SIDE + rows, :] = nxt[...]
    win = QB + 2 * SIDE
    lt64 = _lane_lt64((QB, LANES))
    lane = lax.broadcasted_iota(jnp.int32, (QB, LANES), 1)
    col = lax.broadcasted_iota(jnp.int32, (2 * QB, win), 1)
    nqb = rows // QB
    blocks = [(g, qb, p) for g in range(gb) for qb in range(nqb) for p in range(N_HEADS // 2)]

    def logits(g, qb, p):
        lanes = slice(p * LANES, (p + 1) * LANES)
        qp = q_ref[g, qb * QB:(qb + 1) * QB, lanes]
        zero = jnp.zeros_like(qp)
        q2 = jnp.concatenate([jnp.where(lt64, qp, zero), jnp.where(lt64, zero, qp)], axis=0)
        s = lax.dot_general(q2, kbuf_ref[g, qb * QB:qb * QB + win, lanes], (((1,), (1,)), ((), ())),
                            preferred_element_type=F32) + bias_ref[p]
        if qb == 0:
            s = jnp.where(jnp.logical_and(col < SIDE, first), NEG, s)
        if qb == nqb - 1:
            s = jnp.where(jnp.logical_and(col >= win - SIDE, last), NEG, s)
        return s

    pending = [logits(*blocks[i]) for i in range(min(ATT_PIPE, len(blocks)))]
    lse_pk = None
    for i, (g, qb, p) in enumerate(blocks):
        s = pending.pop(0)
        if i + ATT_PIPE < len(blocks):
            pending.append(logits(*blocks[i + ATT_PIPE]))
        lanes = slice(p * LANES, (p + 1) * LANES)
        qrows = slice(qb * QB, (qb + 1) * QB)
        m = jnp.max(s, axis=1, keepdims=True)
        e = jnp.exp2(s - m)
        l = jnp.sum(e, axis=1, keepdims=True)
        o2 = jnp.dot(e.astype(BF16), vbuf_ref[g, qb * QB:qb * QB + win, lanes], preferred_element_type=F32)
        denom = jnp.where(lt64, l[0:QB], l[QB:2 * QB])
        o_ref[g, qrows, lanes] = jnp.where(lt64, o2[0:QB], o2[QB:2 * QB]) / denom
        lse2 = jnp.where(lt64, m[0:QB], m[QB:2 * QB]) + jnp.log2(denom)
        part = jnp.where(jnp.logical_or(lane == _lse_lane(2 * p), lane == _lse_lane(2 * p + 1)), lse2, 0.0)
        lse_pk = part if p == 0 else lse_pk + part
        if p == N_HEADS // 2 - 1:
            lse_ref[g, qrows, :] = lse_pk


def _attn_branch(q, k, v, bias):
    ng, length, _ = q.shape
    rows = min(RB_ATT, length)
    nblk = length // rows
    gb = math.gcd(ng, max(1, RB_ATT // length))
    hb = rows // SIDE
    nh = length // SIDE
    cur = lambda w: pl.BlockSpec((gb, rows, w), lambda g, j: (g, j, 0))
    prev = pl.BlockSpec((gb, SIDE, D_ATTN), lambda g, j: (g, jnp.maximum(j * hb - 1, 0), 0))
    nxt = pl.BlockSpec((gb, SIDE, D_ATTN), lambda g, j: (g, jnp.minimum((j + 1) * hb, nh - 1), 0))
    return pl.pallas_call(
        functools.partial(_attn_kernel, rows=rows, nblk=nblk, gb=gb),
        grid=(ng // gb, nblk),
        in_specs=[cur(D_ATTN), prev, cur(D_ATTN), nxt, prev, cur(D_ATTN), nxt, _const_spec(bias.shape)],
        out_specs=[cur(D_ATTN), cur(LANES)],
        out_shape=[jax.ShapeDtypeStruct((ng, length, D_ATTN), F32),
                   jax.ShapeDtypeStruct((ng, length, LANES), F32)],
        scratch_shapes=[pltpu.VMEM((gb, rows + 2 * SIDE, D_ATTN), BF16),
                        pltpu.VMEM((gb, rows + 2 * SIDE, D_ATTN), BF16)],
        compiler_params=_params(2), name="dilated_attn",
    )(q, k, k, k, v, v, v, bias)


def _merge_tile(o1_ref, o4_ref, o16_ref, l1_ref, l4_ref, l16_ref, gnorm_ref, dst_ref,
                so4_ref, so16_ref, sl4_ref, sl16_ref, *, tm):
    per = D_ATTN // LANES
    for d, o_ref, l_ref, so_ref, sl_ref in ((4, o4_ref, l4_ref, so4_ref, sl4_ref),
                                            (16, o16_ref, l16_ref, so16_ref, sl16_ref)):
        for r in range(d):
            rows = pl.ds(r, tm // d, stride=d)
            sl_ref[rows, :] = l_ref[r]
            for s in range(per):
                so_ref[s, rows, :] = o_ref[r, :, s * LANES:(s + 1) * LANES]
    lt64 = _lane_lt64((QB, LANES))
    for b in range(tm // QB):
        rs = slice(b * QB, (b + 1) * QB)
        l1, l4, l16 = l1_ref[rs, :], sl4_ref[rs, :], sl16_ref[rs, :]
        m = jnp.maximum(jnp.maximum(l1, l4), l16)
        e1, e4, e16 = jnp.exp2(l1 - m), jnp.exp2(l4 - m), jnp.exp2(l16 - m)
        inv = 1.0 / (e1 + e4 + e16)
        w4, w16 = e4 * inv, e16 * inv
        tiles = []
        for p in range(per):
            lanes = slice(p * LANES, (p + 1) * LANES)
            c0, c1 = _lse_lane(2 * p), _lse_lane(2 * p + 1)
            w4p = jnp.where(lt64, jnp.broadcast_to(w4[:, c0:c0 + 1], (QB, LANES)),
                            jnp.broadcast_to(w4[:, c1:c1 + 1], (QB, LANES)))
            w16p = jnp.where(lt64, jnp.broadcast_to(w16[:, c0:c0 + 1], (QB, LANES)),
                             jnp.broadcast_to(w16[:, c1:c1 + 1], (QB, LANES)))
            o1 = o1_ref[rs, lanes]
            tiles.append(o1 + w4p * (so4_ref[p, rs, :] - o1) + w16p * (so16_ref[p, rs, :] - o1))
        o = jnp.concatenate(tiles, axis=1)
        inv_rms = lax.rsqrt(jnp.mean(o * o, axis=-1, keepdims=True) + EPS)
        dst_ref[rs, :] = (o * inv_rms * gnorm_ref[...]).astype(dst_ref.dtype)


def _layer_norm(x, g, b):
    mu = jnp.mean(x, axis=-1, keepdims=True)
    xc = x - mu
    var = jnp.mean(xc * xc, axis=-1, keepdims=True)
    return xc * lax.rsqrt(var + EPS) * g + b


def _tail_kernel(x_ref, y_ref, o1_ref, o4_ref, o16_ref, l1_ref, l4_ref, l16_ref, gattn_ref,
                 woa_ref, woy_ref, g1_ref, b1_ref, wg_ref, wu_ref, wd_ref, g2_ref, b2_ref,
                 out_ref, attn_ref, so4_ref, so16_ref, sl4_ref, sl16_ref, *, tm):
    t = pl.program_id(0)

    @pl.when(t == 0)
    def _():
        attn_ref[1] = jnp.zeros(attn_ref.shape[1:], attn_ref.dtype)

    dot = functools.partial(jnp.dot, preferred_element_type=F32)
    attn = attn_ref[(t + 1) % 2]
    mix = dot(attn, woa_ref[...]) + dot(y_ref[...], woy_ref[...])
    h1 = _layer_norm(ALPHA * x_ref[...] + mix, g1_ref[...], b1_ref[...])
    h1b = h1.astype(BF16)
    ffn = jnp.zeros_like(h1)
    for c in range(D_FF // FF_CHUNK):
        cols = slice(c * FF_CHUNK, (c + 1) * FF_CHUNK)
        gate = dot(h1b, wg_ref[:, cols])
        up = dot(h1b, wu_ref[:, cols])
        hid = (gate * _sigmoid(gate) * up).astype(BF16)
        ffn = ffn + dot(hid, wd_ref[cols, :])
    out_ref[...] = _layer_norm(ALPHA * h1 + ffn, g2_ref[...], b2_ref[...])

    _merge_tile(o1_ref, o4_ref, o16_ref, l1_ref, l4_ref, l16_ref, gattn_ref, attn_ref.at[t % 2],
                so4_ref, so16_ref, sl4_ref, sl16_ref, tm=tm)


def _tail(x, y, o1, o4, o16, l1, l4, l16, gattn, woa, woy, g1, b1, wg, wu, wd, g2, b2):
    nseq, seq, _ = x.shape
    tm = min(TM_FFN, seq)
    nblk = seq // tm
    ntile = nseq * nblk
    ffn_tile = lambda t: jnp.maximum(t - 1, 0)
    mrg_tile = lambda t: jnp.minimum(t, ntile - 1)
    row = lambda w, tile: pl.BlockSpec((None, tm, w), lambda t: (tile(t) // nblk, tile(t) % nblk, 0))
    res = lambda d, w: pl.BlockSpec((None, d, tm // d, w),
                                    lambda t: (mrg_tile(t) // nblk, 0, mrg_tile(t) % nblk, 0))
    consts = (gattn, woa, woy, g1, b1, wg, wu, wd, g2, b2)
    per = D_ATTN // LANES
    return pl.pallas_call(
        functools.partial(_tail_kernel, tm=tm),
        grid=(ntile + 1,),
        in_specs=[row(D_MODEL, ffn_tile), row(D_SSD, ffn_tile),
                  row(D_ATTN, mrg_tile), res(4, D_ATTN), res(16, D_ATTN),
                  row(LANES, mrg_tile), res(4, LANES), res(16, LANES)]
        + [_const_spec(c.shape) for c in consts],
        out_specs=row(D_MODEL, ffn_tile),
        out_shape=jax.ShapeDtypeStruct((nseq, seq, D_MODEL), F32),
        scratch_shapes=[pltpu.VMEM((2, tm, D_ATTN), BF16),
                        pltpu.VMEM((per, tm, LANES), F32), pltpu.VMEM((per, tm, LANES), F32),
                        pltpu.VMEM((tm, LANES), F32), pltpu.VMEM((tm, LANES), F32)],
        compiler_params=_params(1), name="merge_ffn",
    )(x, y, o1, o4, o16, l1, l4, l16, *consts)


def _t5_bucket(rel):
    half = REL_BUCKETS // 2
    max_exact = half // 2
    ret = (rel > 0).astype(np.int32) * half
    n = np.abs(rel)
    large = max_exact + (np.log(np.maximum(n, 1) / max_exact)
                         / math.log(REL_MAX_DIST / max_exact) * (half - max_exact)).astype(np.int32)
    large = np.minimum(large, half - 1)
    return ret + np.where(n < max_exact, n, large)


def _bias_tables(rel_bias):
    win = QB + 2 * SIDE
    period = win + QB
    tables = []
    for d in DILATIONS:
        buckets = _t5_bucket((np.arange(2 * SIDE + 1) - SIDE) * d)
        band = rel_bias.astype(F32)[buckets].T * LOG2E
        vec = jnp.pad(band, ((0, 0), (0, period - band.shape[1])), constant_values=NEG)
        flat = jnp.tile(vec, (1, QB))[:, :QB * (period - 1)]
        toe = flat.reshape(N_HEADS, QB, period - 1)[:, :, :win]
        tables.append(toe.reshape(N_HEADS // 2, 2 * QB, win))
    return tables


def _selector_constants():
    tri = np.tril(np.ones((CHUNK, CHUNK), np.float32))
    to = lambda a: jnp.asarray(a, BF16)
    return to(tri), to(tri.T)


def _pad_lanes(v, width=LANES):
    return jnp.pad(v, ((0, 0), (0, width - v.shape[1])))


def _prepare(rel_bias, w_in, conv_w, conv_b, dt_bias_fwd, dt_bias_bwd, a_log_fwd, a_log_bwd, d_skip,
             attn_norm_g, ssd_norm_g, w_out, ln1_g, ln1_b, w_gate, w_up, w_down, ln2_g, ln2_b):
    w = w_in[0]
    o = 3 * D_ATTN
    p = {}
    p["wqkv"] = w[:, :o].astype(BF16)
    p["wz"] = w[:, o:o + D_SSD].astype(BF16)
    p["wxbc"] = w[:, o + D_SSD:o + D_SSD + D_XBC].astype(BF16)
    wdt = w[:, o + D_SSD + D_XBC:].astype(BF16)
    p["wdtt"] = wdt.T
    dtb = jnp.concatenate([dt_bias_fwd[0], dt_bias_bwd[0]]).astype(F32)[None, :]
    p["dtbt"] = jnp.broadcast_to(dtb.T, (2 * N_HEADS, LANES))
    alog = jnp.concatenate([a_log_fwd[0], a_log_bwd[0]]).astype(F32)[None, :]
    tri, tri_t = _selector_constants()
    convw = jnp.pad(conv_w[0].astype(F32), ((0, SUBLANES - SSD_CONV), (0, 0)))
    a_tabs = (_pad_lanes(alog), jnp.broadcast_to(alog.T, (2 * N_HEADS, LANES)))
    shifts = np.zeros((SSD_CONV, CHUNK, CHUNK + 2 * HALO_BLK), np.float32)
    for k in range(SSD_CONV):
        shifts[k, np.arange(CHUNK), np.arange(CHUNK) + HALO_BLK + k - SSD_CONV // 2] = 1.0
    p["ssd_fwd"] = (jnp.asarray(shifts, BF16), convw, conv_b[0].astype(F32)[None, :]) + a_tabs + (tri, tri_t)
    p["ssd_bwd"] = a_tabs + (tri_t, tri)
    p["dskip"] = jnp.repeat(d_skip[0].astype(F32), HEAD_DIM)[None, :]
    p["ssd_g"] = ssd_norm_g[0].astype(F32)[None, :]
    p["attn_g"] = attn_norm_g[0].astype(F32)[None, :]
    p["bias"] = _bias_tables(rel_bias)
    wo = w_out[0].astype(BF16)
    p["tail"] = (p["attn_g"], wo[:D_ATTN], wo[D_ATTN:], ln1_g[0].astype(F32)[None, :], ln1_b[0].astype(F32)[None, :],
                w_gate[0].astype(BF16), w_up[0].astype(BF16), w_down[0].astype(BF16),
                ln2_g[0].astype(F32)[None, :], ln2_b[0].astype(F32)[None, :])
    return p


def _encoder_layer(x, p):
    nseq, seq, _ = x.shape
    (q1, k1, v1, q4, k4, v4, q16, k16, v16, z, xbc, dt, dtt) = _in_proj(
        x, p["wqkv"], p["wz"], p["wxbc"], p["wdtt"], p["dtbt"])

    y_ssd = _ssd(xbc, dt, dtt, z, p["ssd_fwd"], p["ssd_bwd"], p["dskip"], p["ssd_g"])

    outs, lses = [], []
    for d, (q, k, v), bias in zip(DILATIONS, ((q1, k1, v1), (q4, k4, v4), (q16, k16, v16)), p["bias"]):
        flat = lambda t: t.reshape(nseq * d, seq // d, D_ATTN)
        o, lse = _attn_branch(flat(q), flat(k), flat(v), bias)
        if d > 1:
            o = o.reshape(nseq, d, seq // d, D_ATTN)
            lse = lse.reshape(nseq, d, seq // d, LANES)
        outs.append(o)
        lses.append(lse)
    return _tail(x, y_ssd, *outs, *lses, *p["tail"])


def kernel(x_prompt, x_sample, rel_bias, w_in, conv_w, conv_b, dt_bias_fwd, dt_bias_bwd, a_log_fwd,
           a_log_bwd, d_skip, attn_norm_g, ssd_norm_g, w_out, ln1_g, ln1_b, w_gate, w_up, w_down,
           ln2_g, ln2_b):
    p = _prepare(rel_bias, w_in, conv_w, conv_b, dt_bias_fwd, dt_bias_bwd, a_log_fwd, a_log_bwd, d_skip,
                 attn_norm_g, ssd_norm_g, w_out, ln1_g, ln1_b, w_gate, w_up, w_down, ln2_g, ln2_b)
    return (_encoder_layer(x_prompt, p), _encoder_layer(x_sample, p))
```
